```python
import jax, jax.numpy as jnp
from jax import lax
import numpy as np

D_MODEL = 2048
BATCH = 8
SEQ = 4096
DEPTH = 4

N_META = 16
BLOCK_Q = 128
ATTN_HEAD_DIM = 128
ATTN_WIDTH = D_MODEL // 2
ATTN_HEADS = ATTN_WIDTH // ATTN_HEAD_DIM
LRU_WIDTH = D_MODEL - ATTN_WIDTH
LRU_HEADS = 8
LRU_HEAD_DIM = LRU_WIDTH // LRU_HEADS
MIX_WIDTH = ATTN_WIDTH + LRU_WIDTH
LRU_CONV = 4
LRU_C = 8.0
IN_COLS = 3 * ATTN_WIDTH + 2 * LRU_WIDTH
FFN_HIDDEN = 5632
FFN_CONV = 3
DEEPNORM_ALPHA = (2 * DEPTH) ** 0.25
DEEPNORM_BETA = (8 * DEPTH) ** -0.25
LN_EPS = 1e-5
RMS_EPS = 1e-6

kernel_name = "hymba_stickbreak_rglru_deepnorm_trunk"


def layer_norm(x, g, b):
    xf = x.astype(jnp.float32)
    mu = jnp.mean(xf, axis=-1, keepdims=True)
    var = jnp.mean(jnp.square(xf - mu), axis=-1, keepdims=True)
    return ((xf - mu) * lax.rsqrt(var + LN_EPS) * g + b).astype(x.dtype)


def rms_norm(x, g):
    xf = x.astype(jnp.float32)
    ms = jnp.mean(jnp.square(xf), axis=-1, keepdims=True)
    return (xf * lax.rsqrt(ms + RMS_EPS) * g).astype(x.dtype)


def causal_dwconv(x, w, b):
    K = w.shape[0]
    T = x.shape[1]
    xp = jnp.pad(x, ((0, 0), (K - 1, 0), (0, 0)))
    y = xp[:, 0:T] * w[0]
    for tap in range(1, K):
        y = y + xp[:, tap:tap + T] * w[tap]
    return y + b


def stick_breaking_attention(q, k, v, n_pad):
    L = q.shape[1]
    scale = ATTN_HEAD_DIM ** -0.5
    outs = []
    for blk in range(L // BLOCK_Q):
        q0 = blk * BLOCK_Q
        kend = q0 + BLOCK_Q
        z = jnp.einsum('bqhd,bkhd->bhqk', q[:, q0:kend], k[:, :kend]).astype(jnp.float32) * scale
        t_idx = q0 + jnp.arange(BLOCK_Q)[:, None]
        s_idx = jnp.arange(kend)[None, :]
        valid = (s_idx < t_idx) & (s_idx >= n_pad)
        log_1m_beta = jnp.where(valid, jax.nn.log_sigmoid(-z), 0.0)
        later = lax.cumsum(log_1m_beta, axis=3, reverse=True) - log_1m_beta
        w = jnp.where(valid, jnp.exp(jax.nn.log_sigmoid(z) + later), 0.0)
        outs.append(jnp.einsum('bhqk,bkhd->bqhd', w.astype(v.dtype), v[:, :kend]))
    return jnp.concatenate(outs, axis=1)


def rg_lru(x, w_a, b_a, w_i, b_i, lam):
    B, T, C = x.shape
    xh = x.reshape(B, T, LRU_HEADS, LRU_HEAD_DIM)
    r = jax.nn.sigmoid(jnp.einsum('bthi,hij->bthj', xh, w_a).reshape(B, T, C) + b_a)
    i = jax.nn.sigmoid(jnp.einsum('bthi,hij->bthj', xh, w_i).reshape(B, T, C) + b_i)
    log_a = (-LRU_C * r * jax.nn.softplus(-lam)).astype(jnp.float32)
    a = jnp.exp(log_a)
    u = jnp.sqrt(-jnp.expm1(2.0 * log_a)) * (i * x).astype(jnp.float32)

    def combine(c1, c2):
        a1, b1 = c1
        a2, b2 = c2
        return a1 * a2, a2 * b1 + b2

    _, h = lax.associative_scan(combine, (a, u), axis=1)
    return h.astype(x.dtype)


def hybrid_mixer(x, w_in, lru_conv_w, lru_conv_b, lru_w_a, lru_b_a, lru_w_i, lru_b_i,
                 lru_lambda, g_attn, g_lru, w_out):
    B, T, _ = x.shape
    proj = x @ w_in
    q, k, v, xr, yg = jnp.split(
        proj, [ATTN_WIDTH, 2 * ATTN_WIDTH, 3 * ATTN_WIDTH, 3 * ATTN_WIDTH + LRU_WIDTH], axis=-1)
    n_pad = (-T) % BLOCK_Q
    pad = ((0, 0), (n_pad, 0), (0, 0), (0, 0))
    heads = lambda t: jnp.pad(t.reshape(B, T, ATTN_HEADS, ATTN_HEAD_DIM), pad)
    o_attn = stick_breaking_attention(heads(q), heads(k), heads(v), n_pad)[:, n_pad:]
    o_attn = o_attn.reshape(B, T, ATTN_WIDTH)
    xc = causal_dwconv(xr, lru_conv_w, lru_conv_b)
    o_lru = rg_lru(xc, lru_w_a, lru_b_a, lru_w_i, lru_b_i, lru_lambda) * jax.nn.gelu(yg)
    merged = jnp.concatenate([rms_norm(o_attn, g_attn), rms_norm(o_lru, g_lru)], axis=-1)
    return merged @ w_out


def conv_glu_ffn(x, w_up, conv_w, conv_b, w_down):
    u = causal_dwconv(x @ w_up, conv_w, conv_b)
    g, val = jnp.split(u, 2, axis=-1)
    return (jax.nn.silu(g) * val) @ w_down


def setup_inputs(seed: int = 0) -> dict:
    key = jax.random.key(seed)
    ks = jax.random.split(key, 24)
    f32 = jnp.float32
    nrm = lambda kk, shape, s: jax.random.normal(kk, shape, f32) * s
    base = jax.random.uniform(ks[9], (DEPTH, LRU_WIDTH), f32, 0.9, 0.999) ** (1.0 / LRU_C)
    return {
        "x": nrm(ks[0], (BATCH, SEQ, D_MODEL), 1.0),
        "meta_tokens": nrm(ks[1], (N_META, D_MODEL), 1.0),
        "ln0_g": 1.0 + nrm(ks[2], (D_MODEL,), 0.02),
        "ln0_b": nrm(ks[3], (D_MODEL,), 0.02),
        "w_in": nrm(ks[4], (DEPTH, D_MODEL, IN_COLS), D_MODEL ** -0.5),
        "lru_conv_w": nrm(ks[5], (DEPTH, LRU_CONV, LRU_WIDTH), LRU_CONV ** -0.5),
        "lru_conv_b": nrm(ks[6], (DEPTH, LRU_WIDTH), 0.01),
        "lru_w_a": nrm(ks[7], (DEPTH, LRU_HEADS, LRU_HEAD_DIM, LRU_HEAD_DIM), LRU_HEAD_DIM ** -0.5),
        "lru_b_a": nrm(ks[8], (DEPTH, LRU_WIDTH), 0.01),
        "lru_w_i": nrm(ks[10], (DEPTH, LRU_HEADS, LRU_HEAD_DIM, LRU_HEAD_DIM), LRU_HEAD_DIM ** -0.5),
        "lru_b_i": nrm(ks[11], (DEPTH, LRU_WIDTH), 0.01),
        "lru_lambda": jnp.log(base) - jnp.log1p(-base),
        "g_attn": 1.0 + nrm(ks[12], (DEPTH, ATTN_WIDTH), 0.02),
        "g_lru": 1.0 + nrm(ks[13], (DEPTH, LRU_WIDTH), 0.02),
        "w_out": nrm(ks[14], (DEPTH, MIX_WIDTH, D_MODEL), DEEPNORM_BETA * MIX_WIDTH ** -0.5),
        "ln1_g": 1.0 + nrm(ks[15], (DEPTH, D_MODEL), 0.02),
        "ln1_b": nrm(ks[16], (DEPTH, D_MODEL), 0.02),
        "ffn_w_up": nrm(ks[17], (DEPTH, D_MODEL, 2 * FFN_HIDDEN), D_MODEL ** -0.5),
        "ffn_conv_w": nrm(ks[18], (DEPTH, FFN_CONV, 2 * FFN_HIDDEN), FFN_CONV ** -0.5),
        "ffn_conv_b": nrm(ks[19], (DEPTH, 2 * FFN_HIDDEN), 0.01),
        "ffn_w_down": nrm(ks[20], (DEPTH, FFN_HIDDEN, D_MODEL), DEEPNORM_BETA * FFN_HIDDEN ** -0.5),
        "ln2_g": 1.0 + nrm(ks[21], (DEPTH, D_MODEL), 0.02),
        "ln2_b": nrm(ks[22], (DEPTH, D_MODEL), 0.02),
    }


def reference(x, meta_tokens, ln0_g, ln0_b, w_in, lru_conv_w, lru_conv_b, lru_w_a, lru_b_a,
              lru_w_i, lru_b_i, lru_lambda, g_attn, g_lru, w_out, ln1_g, ln1_b,
              ffn_w_up, ffn_conv_w, ffn_conv_b, ffn_w_down, ln2_g, ln2_b):
    B = x.shape[0]
    meta = jnp.broadcast_to(meta_tokens[None].astype(x.dtype), (B, N_META, D_MODEL))
    h = layer_norm(jnp.concatenate([meta, x], axis=1), ln0_g, ln0_b)
    for l in range(DEPTH):
        mix = hybrid_mixer(h, w_in[l], lru_conv_w[l], lru_conv_b[l], lru_w_a[l], lru_b_a[l],
                           lru_w_i[l], lru_b_i[l], lru_lambda[l], g_attn[l], g_lru[l], w_out[l])
        h = layer_norm(DEEPNORM_ALPHA * h + mix, ln1_g[l], ln1_b[l])
        ffn = conv_glu_ffn(h, ffn_w_up[l], ffn_conv_w[l], ffn_conv_b[l], ffn_w_down[l])
        h = layer_norm(DEEPNORM_ALPHA * h + ffn, ln2_g[l], ln2_b[l])
    return h[:, N_META:]
```

```python
import functools

import jax
import jax.numpy as jnp
from jax import lax
from jax.experimental import pallas as pl
from jax.experimental.pallas import tpu as pltpu

F32 = jnp.float32
BF16 = jnp.bfloat16

D_MODEL = 2048
DEPTH = 4
N_META = 16
HEAD = 128
ATTN_WIDTH = 1024
LRU_WIDTH = 1024
N_HEADS = 8
IN_COLS = 3 * ATTN_WIDTH + 2 * LRU_WIDTH
FFN_HIDDEN = 5632
LRU_C = 8.0
DEEPNORM_ALPHA = (2 * DEPTH) ** 0.25
LN_EPS = 1e-5
RMS_EPS = 1e-6

CHUNK = 128
EXP_ZERO = 104.0

VMEM_LIMIT = 56 * 1024 * 1024


def _cparams(sem):
    return pltpu.CompilerParams(dimension_semantics=sem, vmem_limit_bytes=VMEM_LIMIT)


def _layer_norm_rows(y, g, b):
    mu = jnp.mean(y, axis=-1, keepdims=True)
    yc = y - mu
    var = jnp.mean(yc * yc, axis=-1, keepdims=True)
    return yc * lax.rsqrt(var + LN_EPS) * g + b


def _softplus(x):
    return jnp.maximum(x, 0.0) + jnp.log(1.0 + jnp.exp(-jnp.abs(x)))


def _ln_kernel(x_ref, g_ref, b_ref, o_ref):
    o_ref[...] = _layer_norm_rows(x_ref[...], g_ref[...], b_ref[...])


def _ln_rows(x, g, b, bm):
    n, d = x.shape
    return pl.pallas_call(
        _ln_kernel,
        grid=(n // bm,),
        in_specs=[pl.BlockSpec((bm, d), lambda i: (i, 0)),
                  pl.BlockSpec((1, d), lambda i: (0, 0)),
                  pl.BlockSpec((1, d), lambda i: (0, 0))],
        out_specs=pl.BlockSpec((bm, d), lambda i: (i, 0)),
        out_shape=jax.ShapeDtypeStruct((n, d), F32),
        compiler_params=_cparams(("arbitrary",)),
        name="ln0",
    )(x, g.reshape(1, d), b.reshape(1, d))


def _inproj_kernel(h_ref, w_ref, o_ref):
    o_ref[...] = jnp.dot(h_ref[...].astype(BF16), w_ref[...],
                         preferred_element_type=F32)


def _inproj(h, w, bm, bn):
    n, d = h.shape
    cols = w.shape[1]
    return pl.pallas_call(
        _inproj_kernel,
        grid=(n // bm, cols // bn),
        in_specs=[pl.BlockSpec((bm, d), lambda i, j: (i, 0)),
                  pl.BlockSpec((d, bn), lambda i, j: (0, j))],
        out_specs=pl.BlockSpec((bm, bn), lambda i, j: (i, j)),
        out_shape=jax.ShapeDtypeStruct((n, cols), F32),
        compiler_params=_cparams(("arbitrary", "arbitrary")),
        name="inproj",
    )(h, w)


def _attn_kernel(q_ref, k_ref, v_ref, o_ref, kb_ref, vb_ref, *, n_blocks):
    kb_ref[...] = k_ref[0].astype(BF16)
    vb_ref[...] = v_ref[0].astype(BF16)
    scale = HEAD ** -0.5

    row = lax.broadcasted_iota(jnp.int32, (CHUNK, CHUNK), 0)
    col = lax.broadcasted_iota(jnp.int32, (CHUNK, CHUNK), 1)
    strictly_earlier = col < row
    jj = lax.broadcasted_iota(jnp.int32, (CHUNK, 2 * CHUNK), 0)
    ss = lax.broadcasted_iota(jnp.int32, (CHUNK, 2 * CHUNK), 1)
    suffix_ones = jnp.where((ss >= CHUNK) | (jj > ss), 1.0, 0.0).astype(BF16)

    def chunk(q, c, carry, acc, diagonal):
        start = pl.multiple_of(c * CHUNK, CHUNK)
        kc = kb_ref[pl.ds(start, CHUNK), :]
        vc = vb_ref[pl.ds(start, CHUNK), :]
        z = lax.dot_general(q, kc, (((1,), (1,)), ((), ())),
                            preferred_element_type=F32) * scale
        sp = _softplus(z)
        l = -sp
        if diagonal:
            l = jnp.where(strictly_earlier, l, 0.0)
        l_hi = l.astype(BF16)
        l_lo = (l - l_hi.astype(F32)).astype(BF16)
        cs = (jnp.dot(l_hi, suffix_ones, preferred_element_type=F32)
              + jnp.dot(l_lo, suffix_ones, preferred_element_type=F32))
        later = cs[:, :CHUNK] + carry
        w = jnp.exp(z - sp + later)
        if diagonal:
            w = jnp.where(strictly_earlier, w, 0.0)
        acc = acc + jnp.dot(w.astype(BF16), vc, preferred_element_type=F32)
        carry = carry + cs[:, CHUNK:]
        return carry, acc

    def qblock(i, _):
        qstart = pl.multiple_of(i * CHUNK, CHUNK)
        q = q_ref[0, pl.ds(qstart, CHUNK), :].astype(BF16)
        zeros = jnp.zeros((CHUNK, CHUNK), F32)
        carry, acc = chunk(q, i, zeros, zeros, True)

        def cond(s):
            c, carry, _ = s
            return jnp.logical_and(c >= 0, jnp.max(carry) > -EXP_ZERO)

        def body(s):
            c, carry, acc = s
            carry, acc = chunk(q, c, carry, acc, False)
            return c - 1, carry, acc

        _, _, acc = lax.while_loop(cond, body, (i - 1, carry, acc))
        o_ref[0, pl.ds(qstart, CHUNK), :] = acc.astype(o_ref.dtype)
        return 0

    lax.fori_loop(0, n_blocks, qblock, 0)


def _attention(proj3, t_pad):
    b = proj3.shape[0]
    blk = (1, t_pad, HEAD)
    return pl.pallas_call(
        functools.partial(_attn_kernel, n_blocks=t_pad // CHUNK),
        grid=(b, N_HEADS),
        in_specs=[pl.BlockSpec(blk, lambda bi, h: (bi, 0, h)),
                  pl.BlockSpec(blk, lambda bi, h: (bi, 0, N_HEADS + h)),
                  pl.BlockSpec(blk, lambda bi, h: (bi, 0, 2 * N_HEADS + h))],
        out_specs=pl.BlockSpec(blk, lambda bi, h: (bi, 0, h)),
        out_shape=jax.ShapeDtypeStruct((b, t_pad, ATTN_WIDTH), F32),
        scratch_shapes=[pltpu.VMEM((t_pad, HEAD), BF16),
                        pltpu.VMEM((t_pad, HEAD), BF16)],
        compiler_params=_cparams(("arbitrary", "arbitrary")),
        name="stickbreak_attn",
    )(proj3, proj3, proj3)


def _shift_rows(x, d, row, fill):
    return jnp.where(row >= d, pltpu.roll(x, d, 0), fill)


def _lru_kernel(xr_ref, yg_ref, cw_ref, cb_ref, wa_ref, ba_ref, wi_ref, bi_ref,
                lam_ref, o_ref, *, t_pad):
    x = xr_ref[0]
    row = lax.broadcasted_iota(jnp.int32, (t_pad, HEAD), 0)
    cw = cw_ref[...]
    xc = x * cw[3:4, :] + cb_ref[...]
    for tap in range(3):
        xc = xc + _shift_rows(x, 3 - tap, row, 0.0) * cw[tap:tap + 1, :]

    xcb = xc.astype(BF16)
    r = jax.nn.sigmoid(jnp.dot(xcb, wa_ref[0].astype(BF16), preferred_element_type=F32)
                       + ba_ref[...])
    gi = jax.nn.sigmoid(jnp.dot(xcb, wi_ref[0].astype(BF16), preferred_element_type=F32)
                        + bi_ref[...])
    log_a = (-LRU_C) * r * _softplus(-lam_ref[...])
    a = jnp.exp(log_a)
    two_log_a = 2.0 * log_a
    series = -two_log_a * (1.0 + two_log_a * (0.5 + two_log_a * (1.0 / 6.0 + two_log_a * (1.0 / 24.0))))
    one_minus = jnp.where(two_log_a > -0.05, series, 1.0 - a * a)
    u = jnp.sqrt(one_minus) * (gi * xc)

    d = 1
    while d < t_pad:
        u = a * _shift_rows(u, d, row, 0.0) + u
        if 2 * d < t_pad:
            a = a * _shift_rows(a, d, row, 1.0)
        d *= 2

    o_ref[0] = u * jax.nn.gelu(yg_ref[0])


def _lru_branch(proj3, cw, cb, wa, ba, wi, bi, lam, t_pad):
    b = proj3.shape[0]
    blk = (1, t_pad, HEAD)
    xr_off = 3 * ATTN_WIDTH // HEAD
    yg_off = xr_off + LRU_WIDTH // HEAD
    vec = lambda a: a.reshape(1, LRU_WIDTH)
    vspec = pl.BlockSpec((1, HEAD), lambda bi_, h: (0, h))
    mspec = pl.BlockSpec((1, HEAD, HEAD), lambda bi_, h: (h, 0, 0))
    return pl.pallas_call(
        functools.partial(_lru_kernel, t_pad=t_pad),
        grid=(b, N_HEADS),
        in_specs=[pl.BlockSpec(blk, lambda bi_, h: (bi_, 0, xr_off + h)),
                  pl.BlockSpec(blk, lambda bi_, h: (bi_, 0, yg_off + h)),
                  pl.BlockSpec((4, HEAD), lambda bi_, h: (0, h)),
                  vspec, mspec, vspec, mspec, vspec, vspec],
        out_specs=pl.BlockSpec(blk, lambda bi_, h: (bi_, 0, h)),
        out_shape=jax.ShapeDtypeStruct((b, t_pad, LRU_WIDTH), F32),
        compiler_params=_cparams(("arbitrary", "arbitrary")),
        name="rglru",
    )(proj3, proj3, cw, vec(cb), wa, vec(ba), wi, vec(bi), vec(lam))


def _rms_rows(x, g):
    ms = jnp.mean(x * x, axis=-1, keepdims=True)
    return x * lax.rsqrt(ms + RMS_EPS) * g


def _outproj_kernel(oa_ref, ol_ref, ga_ref, gl_ref, w_ref, h_ref, g_ref, b_ref, o_ref):
    na = _rms_rows(oa_ref[...], ga_ref[...]).astype(BF16)
    nl = _rms_rows(ol_ref[...], gl_ref[...]).astype(BF16)
    mix = (jnp.dot(na, w_ref[:ATTN_WIDTH, :], preferred_element_type=F32)
           + jnp.dot(nl, w_ref[ATTN_WIDTH:, :], preferred_element_type=F32))
    o_ref[...] = _layer_norm_rows(DEEPNORM_ALPHA * h_ref[...] + mix, g_ref[...], b_ref[...])


def _outproj(oa, ol, ga, gl, w, h, g, b, bm):
    n, d = h.shape
    row = lambda wdt: pl.BlockSpec((bm, wdt), lambda i: (i, 0))
    vec = lambda wdt: pl.BlockSpec((1, wdt), lambda i: (0, 0))
    return pl.pallas_call(
        _outproj_kernel,
        grid=(n // bm,),
        in_specs=[row(ATTN_WIDTH), row(LRU_WIDTH), vec(ATTN_WIDTH), vec(LRU_WIDTH),
                  pl.BlockSpec((d, d), lambda i: (0, 0)), row(d), vec(d), vec(d)],
        out_specs=row(d),
        out_shape=jax.ShapeDtypeStruct((n, d), F32),
        compiler_params=_cparams(("arbitrary",)),
        name="outproj_ln1",
    )(oa, ol, ga.reshape(1, -1), gl.reshape(1, -1), w, h, g.reshape(1, d), b.reshape(1, d))


def _ffn_conv(u, prev8, w_ref, b_ref, row8):
    w = w_ref[...]
    w0, w1, w2 = w[0:1, :], w[1:2, :], w[2:3, :]
    bias = b_ref[...]
    full = u * w2 + pltpu.roll(u, 1, 0) * w1 + pltpu.roll(u, 2, 0) * w0 + bias
    top = u[0:8, :]
    s1 = jnp.where(row8 >= 1, pltpu.roll(top, 1, 0), pltpu.roll(prev8, 1, 0))
    s2 = jnp.where(row8 >= 2, pltpu.roll(top, 2, 0), pltpu.roll(prev8, 2, 0))
    return full, top * w2 + s1 * w1 + s2 * w0 + bias


def _ffn_up_kernel(x_ref, wg_ref, wv_ref, cwg_ref, cwv_ref, cbg_ref, cbv_ref, o_ref,
                   xb_ref, hg_ref, hv_ref, *, tiles_per_seq):
    i = pl.program_id(0)
    j = pl.program_id(1)

    @pl.when(j == 0)
    def _():
        xb_ref[...] = x_ref[...].astype(BF16)

    xb = xb_ref[...]
    ug = jnp.dot(xb, wg_ref[...], preferred_element_type=F32)
    uv = jnp.dot(xb, wv_ref[...], preferred_element_type=F32)
    bm, ck = ug.shape

    @pl.when((i % tiles_per_seq) == 0)
    def _():
        hg_ref[j] = jnp.zeros((8, ck), F32)
        hv_ref[j] = jnp.zeros((8, ck), F32)

    prev_g = hg_ref[j]
    prev_v = hv_ref[j]
    hg_ref[j] = ug[bm - 8:, :]
    hv_ref[j] = uv[bm - 8:, :]
    row8 = lax.broadcasted_iota(jnp.int32, (8, ck), 0)
    cg, cg_top = _ffn_conv(ug, prev_g, cwg_ref, cbg_ref, row8)
    cv, cv_top = _ffn_conv(uv, prev_v, cwv_ref, cbv_ref, row8)
    o_ref[...] = (jax.nn.silu(cg) * cv).astype(o_ref.dtype)
    o_ref[0:8, :] = (jax.nn.silu(cg_top) * cv_top).astype(o_ref.dtype)


def _ffn_up(h, w_up, conv_w, conv_b, bm, ck, tiles_per_seq):
    n, d = h.shape
    nck = FFN_HIDDEN // ck
    cb = conv_b.reshape(1, -1)
    return pl.pallas_call(
        functools.partial(_ffn_up_kernel, tiles_per_seq=tiles_per_seq),
        grid=(n // bm, nck),
        in_specs=[pl.BlockSpec((bm, d), lambda i, j: (i, 0)),
                  pl.BlockSpec((d, ck), lambda i, j: (0, j)),
                  pl.BlockSpec((d, ck), lambda i, j: (0, nck + j)),
                  pl.BlockSpec((3, ck), lambda i, j: (0, j)),
                  pl.BlockSpec((3, ck), lambda i, j: (0, nck + j)),
                  pl.BlockSpec((1, ck), lambda i, j: (0, j)),
                  pl.BlockSpec((1, ck), lambda i, j: (0, nck + j))],
        out_specs=pl.BlockSpec((bm, ck), lambda i, j: (i, j)),
        out_shape=jax.ShapeDtypeStruct((n, FFN_HIDDEN), BF16),
        scratch_shapes=[pltpu.VMEM((bm, d), BF16),
                        pltpu.VMEM((nck, 8, ck), F32),
                        pltpu.VMEM((nck, 8, ck), F32)],
        compiler_params=_cparams(("arbitrary", "arbitrary")),
        name="ffn_up_conv_glu",
    )(h, w_up, w_up, conv_w, conv_w, cb, cb)


def _ffn_down_kernel(a_ref, w_ref, h_ref, g_ref, b_ref, o_ref):
    k = pl.program_id(1)
    part = jnp.dot(a_ref[...], w_ref[...], preferred_element_type=F32)

    @pl.when(k == 0)
    def _():
        o_ref[...] = part

    @pl.when(k > 0)
    def _():
        o_ref[...] += part

    @pl.when(k == pl.num_programs(1) - 1)
    def _():
        o_ref[...] = _layer_norm_rows(DEEPNORM_ALPHA * h_ref[...] + o_ref[...],
                                      g_ref[...], b_ref[...])


def _ffn_down(act, w, h, g, b, bm, kc):
    n, d = h.shape
    return pl.pallas_call(
        _ffn_down_kernel,
        grid=(n // bm, FFN_HIDDEN // kc),
        in_specs=[pl.BlockSpec((bm, kc), lambda i, k: (i, k)),
                  pl.BlockSpec((kc, d), lambda i, k: (k, 0)),
                  pl.BlockSpec((bm, d), lambda i, k: (i, 0)),
                  pl.BlockSpec((1, d), lambda i, k: (0, 0)),
                  pl.BlockSpec((1, d), lambda i, k: (0, 0))],
        out_specs=pl.BlockSpec((bm, d), lambda i, k: (i, 0)),
        out_shape=jax.ShapeDtypeStruct((n, d), F32),
        compiler_params=_cparams(("arbitrary", "arbitrary")),
        name="ffn_down_ln2",
    )(act, w, h, g.reshape(1, d), b.reshape(1, d))


def kernel(x, meta_tokens, ln0_g, ln0_b, w_in, lru_conv_w, lru_conv_b, lru_w_a, lru_b_a, lru_w_i, lru_b_i, lru_lambda, g_attn, g_lru, w_out, ln1_g, ln1_b, ffn_w_up, ffn_conv_w, ffn_conv_b, ffn_w_down, ln2_g, ln2_b):
    b, seq, d = x.shape
    t = N_META + seq
    t_pad = -(-t // CHUNK) * CHUNK
    tiles_per_seq = 4
    bm = t_pad // tiles_per_seq
    n = b * t_pad

    meta = jnp.broadcast_to(meta_tokens[None].astype(x.dtype), (b, N_META, d))
    tail = jnp.zeros((b, t_pad - t, d), x.dtype)
    h = jnp.concatenate([meta, x, tail], axis=1).reshape(n, d)
    h = _ln_rows(h, ln0_g, ln0_b, bm // 2)

    for l in range(DEPTH):
        proj = _inproj(h, w_in[l].astype(BF16), bm, 1024)
        proj3 = proj.reshape(b, t_pad, IN_COLS)
        o_attn = _attention(proj3, t_pad)
        o_lru = _lru_branch(proj3, lru_conv_w[l], lru_conv_b[l], lru_w_a[l], lru_b_a[l],
                            lru_w_i[l], lru_b_i[l], lru_lambda[l], t_pad)
        h = _outproj(o_attn.reshape(n, ATTN_WIDTH), o_lru.reshape(n, LRU_WIDTH),
                     g_attn[l], g_lru[l], w_out[l].astype(BF16), h, ln1_g[l], ln1_b[l], bm // 2)
        act = _ffn_up(h, ffn_w_up[l].astype(BF16), ffn_conv_w[l], ffn_conv_b[l],
                      bm, 512, tiles_per_seq)
        h = _ffn_down(act, ffn_w_down[l].astype(BF16), h, ln2_g[l], ln2_b[l], bm // 2, 1408)

    return h.reshape(b, t_pad, d)[:, N_META:t]
```

```python
import functools

import jax
import jax.numpy as jnp
from jax import lax
from jax.experimental import pallas as pl
from jax.experimental.pallas import tpu as pltpu

F32 = jnp.float32
BF16 = jnp.bfloat16

D_MODEL = 2048
DEPTH = 4
N_META = 16
HEAD = 128
ATTN_WIDTH = 1024
LRU_WIDTH = 1024
N_HEADS = 8
IN_COLS = 3 * ATTN_WIDTH + 2 * LRU_WIDTH
FFN_HIDDEN = 5632
LRU_C = 8.0
DEEPNORM_ALPHA = (2 * DEPTH) ** 0.25
LN_EPS = 1e-5
RMS_EPS = 1e-6

CHUNK = 128
EXP_ZERO = 104.0
WINDOW = 3
GROUP = 6

LN_SLAB = 48

VMEM_LIMIT = 56 * 1024 * 1024


def _cparams(sem):
    return pltpu.CompilerParams(dimension_semantics=sem, vmem_limit_bytes=VMEM_LIMIT)


def _layer_norm_rows(y, g, b):
    mu = jnp.mean(y, axis=-1, keepdims=True)
    yc = y - mu
    var = jnp.mean(yc * yc, axis=-1, keepdims=True)
    return yc * lax.rsqrt(var + LN_EPS) * g + b


def _softplus(x):
    return jnp.maximum(x, 0.0) + jnp.log(1.0 + jnp.exp(-jnp.abs(x)))


def _stream_shapes(n, d):
    return (jax.ShapeDtypeStruct((n, d), F32), jax.ShapeDtypeStruct((n, d), BF16))


def _ln_kernel(x_ref, g_ref, b_ref, o_ref, ob_ref):
    y = _layer_norm_rows(x_ref[...], g_ref[...], b_ref[...])
    o_ref[...] = y
    ob_ref[...] = y.astype(BF16)


def _ln_rows(x, g, b, bm):
    n, d = x.shape
    row = pl.BlockSpec((bm, d), lambda i: (i, 0))
    vec = pl.BlockSpec((1, d), lambda i: (0, 0))
    return pl.pallas_call(
        _ln_kernel,
        grid=(n // bm,),
        in_specs=[row, vec, vec],
        out_specs=(row, row),
        out_shape=_stream_shapes(n, d),
        compiler_params=_cparams(("arbitrary",)),
        name="ln0",
    )(x, g.reshape(1, d), b.reshape(1, d))


def _inproj_kernel(h_ref, w_ref, o_ref):
    o_ref[...] = jnp.dot(h_ref[...], w_ref[...], preferred_element_type=F32)


def _inproj(hb, w, bm, bn):
    n, d = hb.shape
    cols = w.shape[1]
    return pl.pallas_call(
        _inproj_kernel,
        grid=(n // bm, cols // bn),
        in_specs=[pl.BlockSpec((bm, d), lambda i, j: (i, 0)),
                  pl.BlockSpec((d, bn), lambda i, j: (0, j))],
        out_specs=pl.BlockSpec((bm, bn), lambda i, j: (i, j)),
        out_shape=jax.ShapeDtypeStruct((n, cols), F32),
        compiler_params=_cparams(("arbitrary", "arbitrary")),
        name="inproj",
    )(hb, w)


def _attn_kernel(q_ref, k_ref, v_ref, o_ref, kb_ref, vb_ref, *, n_blocks):
    kb_ref[...] = k_ref[0].astype(BF16)
    vb_ref[...] = v_ref[0].astype(BF16)
    scale = HEAD ** -0.5

    row = lax.broadcasted_iota(jnp.int32, (CHUNK, CHUNK), 0)
    col = lax.broadcasted_iota(jnp.int32, (CHUNK, CHUNK), 1)
    strictly_earlier = col < row
    jj = lax.broadcasted_iota(jnp.int32, (2 * CHUNK, 2 * CHUNK), 0)
    ss = lax.broadcasted_iota(jnp.int32, (2 * CHUNK, 2 * CHUNK), 1)
    suffix_ones = jnp.where((ss >= CHUNK) | ((jj & (CHUNK - 1)) > ss), 1.0, 0.0).astype(BF16)

    def log_survival(z, diagonal):
        sp = _softplus(z)
        l = -sp
        if diagonal:
            l = jnp.where(strictly_earlier, l, 0.0)
        l_hi = l.astype(BF16)
        l_lo = (l - l_hi.astype(F32)).astype(BF16)
        cs = jnp.dot(jnp.concatenate([l_hi, l_lo], axis=1), suffix_ones,
                     preferred_element_type=F32)
        return sp, cs

    def weights(z, sp, later, diagonal):
        w = jnp.exp(z - sp + later)
        if diagonal:
            w = jnp.where(strictly_earlier, w, 0.0)
        return w.astype(BF16)

    def windows(blocks, nwin):
        qs, vws, zs = [], [], []
        for i in blocks:
            qstart = i * CHUNK
            kstart = (i - nwin + 1) * CHUNK
            if not isinstance(i, int):
                qstart = pl.multiple_of(qstart, CHUNK)
                kstart = pl.multiple_of(kstart, CHUNK)
            q = q_ref[0, pl.ds(qstart, CHUNK), :].astype(BF16)
            kw = kb_ref[pl.ds(kstart, nwin * CHUNK), :]
            vws.append(vb_ref[pl.ds(kstart, nwin * CHUNK), :])
            z = lax.dot_general(q, kw, (((1,), (1,)), ((), ())),
                                preferred_element_type=F32) * scale
            qs.append(q)
            zs.append([z[:, c * CHUNK:(c + 1) * CHUNK] for c in range(nwin)])
        stats = [[log_survival(z[c], c == nwin - 1) for c in range(nwin)] for z in zs]
        carries, ws = [], []
        for z, stat in zip(zs, stats):
            carry = None
            w = [None] * nwin
            for c in reversed(range(nwin)):
                sp, cs = stat[c]
                later = cs[:, :CHUNK] if carry is None else cs[:, :CHUNK] + carry
                w[c] = weights(z[c], sp, later, c == nwin - 1)
                carry = cs[:, CHUNK:] if carry is None else carry + cs[:, CHUNK:]
            carries.append(carry)
            ws.append(w[0] if nwin == 1 else jnp.concatenate(w, axis=1))
        accs = [jnp.dot(w, vw, preferred_element_type=F32) for w, vw in zip(ws, vws)]
        return list(zip(qs, carries, accs))

    def far_chunks(q, c0, carry, acc):
        def cond(s):
            c, carry, _ = s
            return jnp.logical_and(c >= 0, jnp.max(carry) > -EXP_ZERO)

        def body(s):
            c, carry, acc = s
            start = pl.multiple_of(c * CHUNK, CHUNK)
            z = lax.dot_general(q, kb_ref[pl.ds(start, CHUNK), :], (((1,), (1,)), ((), ())),
                                preferred_element_type=F32) * scale
            sp, cs = log_survival(z, False)
            w = weights(z, sp, cs[:, :CHUNK] + carry, False)
            acc = acc + jnp.dot(w, vb_ref[pl.ds(start, CHUNK), :], preferred_element_type=F32)
            return c - 1, carry + cs[:, CHUNK:], acc

        return lax.while_loop(cond, body, (c0, carry, acc))[2]

    def store(i, acc):
        start = i * CHUNK if isinstance(i, int) else pl.multiple_of(i * CHUNK, CHUNK)
        o_ref[0, pl.ds(start, CHUNK), :] = acc.astype(o_ref.dtype)

    for i in range(WINDOW):
        store(i, windows([i], i + 1)[0][2])

    def group(g, _):
        blocks = [WINDOW + g * GROUP + b for b in range(GROUP)]
        done = windows(blocks, WINDOW)
        worst = done[0][1]
        for _, carry, _ in done[1:]:
            worst = jnp.maximum(worst, carry)
        for i, (_, _, acc) in zip(blocks, done):
            store(i, acc)

        @pl.when(jnp.max(worst) > -EXP_ZERO)
        def _():
            for i, (q, carry, acc) in zip(blocks, done):
                store(i, far_chunks(q, i - WINDOW, carry, acc))

        return 0

    lax.fori_loop(0, (n_blocks - WINDOW) // GROUP, group, 0)


def _attention(proj3, t_pad):
    b = proj3.shape[0]
    n_blocks = t_pad // CHUNK
    assert n_blocks >= WINDOW and (n_blocks - WINDOW) % GROUP == 0
    blk = (1, t_pad, HEAD)
    return pl.pallas_call(
        functools.partial(_attn_kernel, n_blocks=n_blocks),
        grid=(b, N_HEADS),
        in_specs=[pl.BlockSpec(blk, lambda bi, h: (bi, 0, h)),
                  pl.BlockSpec(blk, lambda bi, h: (bi, 0, N_HEADS + h)),
                  pl.BlockSpec(blk, lambda bi, h: (bi, 0, 2 * N_HEADS + h))],
        out_specs=pl.BlockSpec(blk, lambda bi, h: (bi, 0, h)),
        out_shape=jax.ShapeDtypeStruct((b, t_pad, ATTN_WIDTH), F32),
        scratch_shapes=[pltpu.VMEM((t_pad, HEAD), BF16),
                        pltpu.VMEM((t_pad, HEAD), BF16)],
        compiler_params=_cparams(("arbitrary", "arbitrary")),
        name="stickbreak_attn",
    )(proj3, proj3, proj3)


def _shift_rows(x, d, row, fill):
    return jnp.where(row >= d, pltpu.roll(x, d, 0), fill)


def _lru_kernel(xr_ref, yg_ref, cw_ref, cb_ref, wa_ref, ba_ref, wi_ref, bi_ref,
                lam_ref, o_ref, *, t_pad):
    x = xr_ref[0]
    row = lax.broadcasted_iota(jnp.int32, (t_pad, HEAD), 0)
    cw = cw_ref[...]
    xc = x * cw[3:4, :] + cb_ref[...]
    for tap in range(3):
        xc = xc + _shift_rows(x, 3 - tap, row, 0.0) * cw[tap:tap + 1, :]

    xcb = xc.astype(BF16)
    r = jax.nn.sigmoid(jnp.dot(xcb, wa_ref[0].astype(BF16), preferred_element_type=F32)
                       + ba_ref[...])
    gi = jax.nn.sigmoid(jnp.dot(xcb, wi_ref[0].astype(BF16), preferred_element_type=F32)
                        + bi_ref[...])
    log_a = (-LRU_C) * r * _softplus(-lam_ref[...])
    a = jnp.exp(log_a)
    two_log_a = 2.0 * log_a
    series = -two_log_a * (1.0 + two_log_a * (0.5 + two_log_a * (1.0 / 6.0 + two_log_a * (1.0 / 24.0))))
    one_minus = jnp.where(two_log_a > -0.05, series, 1.0 - a * a)
    u = jnp.sqrt(one_minus) * (gi * xc)

    d = 1
    while d < t_pad:
        u = a * _shift_rows(u, d, row, 0.0) + u
        if 2 * d < t_pad:
            a = a * _shift_rows(a, d, row, 1.0)
        d *= 2

    o_ref[0] = u * jax.nn.gelu(yg_ref[0])


def _lru_branch(proj3, cw, cb, wa, ba, wi, bi, lam, t_pad):
    b = proj3.shape[0]
    blk = (1, t_pad, HEAD)
    xr_off = 3 * ATTN_WIDTH // HEAD
    yg_off = xr_off + LRU_WIDTH // HEAD
    vec = lambda a: a.reshape(1, LRU_WIDTH)
    vspec = pl.BlockSpec((1, HEAD), lambda bi_, h: (0, h))
    mspec = pl.BlockSpec((1, HEAD, HEAD), lambda bi_, h: (h, 0, 0))
    return pl.pallas_call(
        functools.partial(_lru_kernel, t_pad=t_pad),
        grid=(b, N_HEADS),
        in_specs=[pl.BlockSpec(blk, lambda bi_, h: (bi_, 0, xr_off + h)),
                  pl.BlockSpec(blk, lambda bi_, h: (bi_, 0, yg_off + h)),
                  pl.BlockSpec((4, HEAD), lambda bi_, h: (0, h)),
                  vspec, mspec, vspec, mspec, vspec, vspec],
        out_specs=pl.BlockSpec(blk, lambda bi_, h: (bi_, 0, h)),
        out_shape=jax.ShapeDtypeStruct((b, t_pad, LRU_WIDTH), F32),
        compiler_params=_cparams(("arbitrary", "arbitrary")),
        name="rglru",
    )(proj3, proj3, cw, vec(cb), wa, vec(ba), wi, vec(bi), vec(lam))


def _rms_rows(x, g):
    ms = jnp.mean(x * x, axis=-1, keepdims=True)
    return x * lax.rsqrt(ms + RMS_EPS) * g


def _outproj_kernel(oa_ref, ol_ref, ga_ref, gl_ref, w_ref, h_ref, g_ref, b_ref, o_ref, ob_ref):
    na = _rms_rows(oa_ref[...], ga_ref[...]).astype(BF16)
    nl = _rms_rows(ol_ref[...], gl_ref[...]).astype(BF16)
    mix = (jnp.dot(na, w_ref[:ATTN_WIDTH, :], preferred_element_type=F32)
           + jnp.dot(nl, w_ref[ATTN_WIDTH:, :], preferred_element_type=F32))
    y = _layer_norm_rows(DEEPNORM_ALPHA * h_ref[...] + mix, g_ref[...], b_ref[...])
    o_ref[...] = y
    ob_ref[...] = y.astype(BF16)


def _outproj(oa, ol, ga, gl, w, h, g, b, bm):
    n, d = h.shape
    row = lambda wdt: pl.BlockSpec((bm, wdt), lambda i: (i, 0))
    vec = lambda wdt: pl.BlockSpec((1, wdt), lambda i: (0, 0))
    return pl.pallas_call(
        _outproj_kernel,
        grid=(n // bm,),
        in_specs=[row(ATTN_WIDTH), row(LRU_WIDTH), vec(ATTN_WIDTH), vec(LRU_WIDTH),
                  pl.BlockSpec((d, d), lambda i: (0, 0)), row(d), vec(d), vec(d)],
        out_specs=(row(d), row(d)),
        out_shape=_stream_shapes(n, d),
        compiler_params=_cparams(("arbitrary",)),
        name="outproj_ln1",
    )(oa, ol, ga.reshape(1, -1), gl.reshape(1, -1), w, h, g.reshape(1, d), b.reshape(1, d))


def _ffn_conv(u, prev8, w_ref, b_ref, row8):
    w = w_ref[...]
    w0, w1, w2 = w[0:1, :], w[1:2, :], w[2:3, :]
    bias = b_ref[...]
    full = u * w2 + pltpu.roll(u, 1, 0) * w1 + pltpu.roll(u, 2, 0) * w0 + bias
    top = u[0:8, :]
    s1 = jnp.where(row8 >= 1, pltpu.roll(top, 1, 0), pltpu.roll(prev8, 1, 0))
    s2 = jnp.where(row8 >= 2, pltpu.roll(top, 2, 0), pltpu.roll(prev8, 2, 0))
    return full, top * w2 + s1 * w1 + s2 * w0 + bias


def _ffn_up_kernel(x_ref, wg_ref, wv_ref, cwg_ref, cwv_ref, cbg_ref, cbv_ref, o_ref,
                   ug_ref, uv_ref, hg_ref, hv_ref, *, nck, tiles_per_seq):
    s = pl.program_id(0)

    @pl.when(s == 0)
    def _():
        ug_ref[...] = jnp.zeros_like(ug_ref)
        uv_ref[...] = jnp.zeros_like(uv_ref)
        hg_ref[...] = jnp.zeros_like(hg_ref)
        hv_ref[...] = jnp.zeros_like(hv_ref)

    prev = jnp.maximum(s - 1, 0)
    ip = prev // nck
    jp = prev % nck
    ug = ug_ref[...]
    uv = uv_ref[...]
    bm, ck = ug.shape
    seq_start = (ip % tiles_per_seq) == 0
    prev_g = jnp.where(seq_start, 0.0, hg_ref[jp])
    prev_v = jnp.where(seq_start, 0.0, hv_ref[jp])
    hg_ref[jp] = ug[bm - 8:, :]
    hv_ref[jp] = uv[bm - 8:, :]
    row8 = lax.broadcasted_iota(jnp.int32, (8, ck), 0)
    cg, cg_top = _ffn_conv(ug, prev_g, cwg_ref, cbg_ref, row8)
    cv, cv_top = _ffn_conv(uv, prev_v, cwv_ref, cbv_ref, row8)
    o_ref[...] = (jax.nn.silu(cg) * cv).astype(o_ref.dtype)
    o_ref[0:8, :] = (jax.nn.silu(cg_top) * cv_top).astype(o_ref.dtype)

    x = x_ref[...]
    ug_ref[...] = jnp.dot(x, wg_ref[...], preferred_element_type=F32)
    uv_ref[...] = jnp.dot(x, wv_ref[...], preferred_element_type=F32)


def _ffn_up(hb, w_up, conv_w, conv_b, bm, ck, tiles_per_seq):
    n, d = hb.shape
    nck = FFN_HIDDEN // ck
    steps = (n // bm) * nck
    cb = conv_b.reshape(1, -1)
    cur = lambda s: jnp.minimum(s, steps - 1)
    prev = lambda s: jnp.maximum(s - 1, 0)
    return pl.pallas_call(
        functools.partial(_ffn_up_kernel, nck=nck, tiles_per_seq=tiles_per_seq),
        grid=(steps + 1,),
        in_specs=[pl.BlockSpec((bm, d), lambda s: (cur(s) // nck, 0)),
                  pl.BlockSpec((d, ck), lambda s: (0, cur(s) % nck)),
                  pl.BlockSpec((d, ck), lambda s: (0, nck + cur(s) % nck)),
                  pl.BlockSpec((3, ck), lambda s: (0, prev(s) % nck)),
                  pl.BlockSpec((3, ck), lambda s: (0, nck + prev(s) % nck)),
                  pl.BlockSpec((1, ck), lambda s: (0, prev(s) % nck)),
                  pl.BlockSpec((1, ck), lambda s: (0, nck + prev(s) % nck))],
        out_specs=pl.BlockSpec((bm, ck), lambda s: (prev(s) // nck, prev(s) % nck)),
        out_shape=jax.ShapeDtypeStruct((n, FFN_HIDDEN), BF16),
        scratch_shapes=[pltpu.VMEM((bm, ck), F32),
                        pltpu.VMEM((bm, ck), F32),
                        pltpu.VMEM((nck, 8, ck), F32),
                        pltpu.VMEM((nck, 8, ck), F32)],
        compiler_params=_cparams(("arbitrary",)),
        name="ffn_up_conv_glu",
    )(hb, w_up, w_up, conv_w, conv_w, cb, cb)


def _ffn_down_kernel(a_ref, w_ref, h_ref, g_ref, b_ref, o_ref, ob_ref):
    k = pl.program_id(1)
    part = jnp.dot(a_ref[...], w_ref[...], preferred_element_type=F32)

    @pl.when(k == 0)
    def _():
        o_ref[...] = part

    @pl.when(k == pl.num_programs(1) - 1)
    def _():
        o_ref[...] += part
        for r in range(0, o_ref.shape[0], LN_SLAB):
            rows = pl.ds(r, LN_SLAB)
            y = _layer_norm_rows(DEEPNORM_ALPHA * h_ref[rows, :] + o_ref[rows, :],
                                 g_ref[...], b_ref[...])
            o_ref[rows, :] = y
            ob_ref[rows, :] = y.astype(BF16)


def _ffn_down(act, w, h, g, b, bm):
    n, d = h.shape
    assert bm % LN_SLAB == 0
    kc = FFN_HIDDEN // 2
    row = pl.BlockSpec((bm, d), lambda i, k: (i, 0))
    vec = pl.BlockSpec((1, d), lambda i, k: (0, 0))
    return pl.pallas_call(
        _ffn_down_kernel,
        grid=(n // bm, 2),
        in_specs=[pl.BlockSpec((bm, kc), lambda i, k: (i, k)),
                  pl.BlockSpec((kc, d), lambda i, k: (k, 0)),
                  pl.BlockSpec((bm, d), lambda i, k: (i, 0), pipeline_mode=pl.Buffered(1)),
                  vec, vec],
        out_specs=(row, row),
        out_shape=_stream_shapes(n, d),
        compiler_params=_cparams(("arbitrary", "arbitrary")),
        name="ffn_down_ln2",
    )(act, w, h, g.reshape(1, d), b.reshape(1, d))


def kernel(x, meta_tokens, ln0_g, ln0_b, w_in, lru_conv_w, lru_conv_b, lru_w_a, lru_b_a, lru_w_i, lru_b_i, lru_lambda, g_attn, g_lru, w_out, ln1_g, ln1_b, ffn_w_up, ffn_conv_w, ffn_conv_b, ffn_w_down, ln2_g, ln2_b):
    b, seq, d = x.shape
    t = N_META + seq
    t_pad = -(-t // CHUNK) * CHUNK
    tiles_per_seq = 4
    bm = t_pad // tiles_per_seq
    n = b * t_pad

    meta = jnp.broadcast_to(meta_tokens[None].astype(x.dtype), (b, N_META, d))
    tail = jnp.zeros((b, t_pad - t, d), x.dtype)
    h = jnp.concatenate([meta, x, tail], axis=1).reshape(n, d)
    h, hb = _ln_rows(h, ln0_g, ln0_b, bm // 2)

    for l in range(DEPTH):
        proj = _inproj(hb, w_in[l].astype(BF16), bm, 1024)
        proj3 = proj.reshape(b, t_pad, IN_COLS)
        o_attn = _attention(proj3, t_pad)
        o_lru = _lru_branch(proj3, lru_conv_w[l], lru_conv_b[l], lru_w_a[l], lru_b_a[l],
                            lru_w_i[l], lru_b_i[l], lru_lambda[l], t_pad)
        h, hb = _outproj(o_attn.reshape(n, ATTN_WIDTH), o_lru.reshape(n, LRU_WIDTH),
                         g_attn[l], g_lru[l], w_out[l].astype(BF16), h, ln1_g[l], ln1_b[l], bm // 2)
        act = _ffn_up(hb, ffn_w_up[l].astype(BF16), ffn_conv_w[l], ffn_conv_b[l],
                      bm, 512, tiles_per_seq)
        h, hb = _ffn_down(act, ffn_w_down[l].astype(BF16), h, ln2_g[l], ln2_b[l], bm // 2)

    return h.reshape(b, t_pad, d)[:, N_META:t]
```

```python
import functools

import jax
import jax.numpy as jnp
from jax import lax
from jax.experimental import pallas as pl
from jax.experimental.pallas import tpu as pltpu

F32 = jnp.float32
BF16 = jnp.bfloat16

D_MODEL = 2048
DEPTH = 4
N_META = 16
HEAD = 128
ATTN_WIDTH = 1024
LRU_WIDTH = 1024
N_HEADS = 8
IN_COLS = 3 * ATTN_WIDTH + 2 * LRU_WIDTH
FFN_HIDDEN = 5632
LRU_C = 8.0
DEEPNORM_ALPHA = (2 * DEPTH) ** 0.25
LN_EPS = 1e-5
RMS_EPS = 1e-6

CHUNK = 128
EXP_ZERO = 104.0
WINDOW = 3
GROUP = 6
LN_SLAB = 32

VMEM_LIMIT = 56 * 1024 * 1024


def _cparams(sem):
    return pltpu.CompilerParams(dimension_semantics=sem, vmem_limit_bytes=VMEM_LIMIT)


def _layer_norm_rows(y, g, b):
    mu = jnp.mean(y, axis=-1, keepdims=True)
    yc = y - mu
    var = jnp.mean(yc * yc, axis=-1, keepdims=True)
    return yc * lax.rsqrt(var + LN_EPS) * g + b


def _softplus(x):
    return jnp.maximum(x, 0.0) + jnp.log(1.0 + jnp.exp(-jnp.abs(x)))


def _stream_shapes(n, d):
    return (jax.ShapeDtypeStruct((n, d), F32), jax.ShapeDtypeStruct((n, d), BF16))


def _ln_kernel(x_ref, g_ref, b_ref, o_ref, ob_ref):
    y = _layer_norm_rows(x_ref[...], g_ref[...], b_ref[...])
    o_ref[...] = y
    ob_ref[...] = y.astype(BF16)


def _ln_rows(x, g, b, bm):
    n, d = x.shape
    row = pl.BlockSpec((bm, d), lambda i: (i, 0))
    vec = pl.BlockSpec((1, d), lambda i: (0, 0))
    return pl.pallas_call(
        _ln_kernel,
        grid=(n // bm,),
        in_specs=[row, vec, vec],
        out_specs=(row, row),
        out_shape=_stream_shapes(n, d),
        compiler_params=_cparams(("arbitrary",)),
        name="ln0",
    )(x, g.reshape(1, d), b.reshape(1, d))


def _inproj_kernel(h_ref, w_ref, o_ref):
    o_ref[...] = jnp.dot(h_ref[...], w_ref[...], preferred_element_type=F32)


def _inproj(hb, w, l, bm, bn):
    n, d = hb.shape
    cols = w.shape[-1]
    return pl.pallas_call(
        _inproj_kernel,
        grid=(n // bm, cols // bn),
        in_specs=[pl.BlockSpec((bm, d), lambda i, j: (i, 0)),
                  pl.BlockSpec((None, d, bn), lambda i, j: (l, 0, j))],
        out_specs=pl.BlockSpec((bm, bn), lambda i, j: (i, j)),
        out_shape=jax.ShapeDtypeStruct((n, cols), F32),
        compiler_params=_cparams(("arbitrary", "arbitrary")),
        name="inproj",
    )(hb, w)


def _attn_kernel(q_ref, k_ref, v_ref, o_ref, kb_ref, vb_ref, *, n_blocks):
    kb_ref[...] = k_ref[0].astype(BF16)
    vb_ref[...] = v_ref[0].astype(BF16)
    scale = HEAD ** -0.5

    row = lax.broadcasted_iota(jnp.int32, (CHUNK, CHUNK), 0)
    col = lax.broadcasted_iota(jnp.int32, (CHUNK, CHUNK), 1)
    strictly_earlier = col < row
    jj = lax.broadcasted_iota(jnp.int32, (2 * CHUNK, 2 * CHUNK), 0)
    ss = lax.broadcasted_iota(jnp.int32, (2 * CHUNK, 2 * CHUNK), 1)
    suffix_ones = jnp.where((ss >= CHUNK) | ((jj & (CHUNK - 1)) > ss), 1.0, 0.0).astype(BF16)

    def log_survival(z, diagonal):
        sp = _softplus(z)
        l = -sp
        if diagonal:
            l = jnp.where(strictly_earlier, l, 0.0)
        l_hi = l.astype(BF16)
        l_lo = (l - l_hi.astype(F32)).astype(BF16)
        cs = jnp.dot(jnp.concatenate([l_hi, l_lo], axis=1), suffix_ones,
                     preferred_element_type=F32)
        return sp, cs

    def weights(z, sp, later, diagonal):
        w = jnp.exp(z - sp + later)
        if diagonal:
            w = jnp.where(strictly_earlier, w, 0.0)
        return w.astype(BF16)

    def windows(blocks, nwins):
        qs, vws, zs = [], [], []
        for i, nwin in zip(blocks, nwins):
            qstart = i * CHUNK
            kstart = (i - nwin + 1) * CHUNK
            if not isinstance(i, int):
                qstart = pl.multiple_of(qstart, CHUNK)
                kstart = pl.multiple_of(kstart, CHUNK)
            q = q_ref[0, pl.ds(qstart, CHUNK), :].astype(BF16)
            kw = kb_ref[pl.ds(kstart, nwin * CHUNK), :]
            vws.append(vb_ref[pl.ds(kstart, nwin * CHUNK), :])
            z = lax.dot_general(q, kw, (((1,), (1,)), ((), ())),
                                preferred_element_type=F32) * scale
            qs.append(q)
            zs.append([z[:, c * CHUNK:(c + 1) * CHUNK] for c in range(nwin)])
        stats = [[log_survival(zc, c == len(z) - 1) for c, zc in enumerate(z)] for z in zs]
        carries, ws = [], []
        for z, stat in zip(zs, stats):
            nwin = len(z)
            carry = None
            w = [None] * nwin
            for c in reversed(range(nwin)):
                sp, cs = stat[c]
                later = cs[:, :CHUNK] if carry is None else cs[:, :CHUNK] + carry
                w[c] = weights(z[c], sp, later, c == nwin - 1)
                carry = cs[:, CHUNK:] if carry is None else carry + cs[:, CHUNK:]
            carries.append(carry)
            ws.append(w[0] if nwin == 1 else jnp.concatenate(w, axis=1))
        accs = [jnp.dot(w, vw, preferred_element_type=F32) for w, vw in zip(ws, vws)]
        return list(zip(qs, carries, accs))

    def far_chunks(q, c0, carry, acc):
        def cond(s):
            c, carry, _ = s
            return jnp.logical_and(c >= 0, jnp.max(carry) > -EXP_ZERO)

        def body(s):
            c, carry, acc = s
            start = pl.multiple_of(c * CHUNK, CHUNK)
            z = lax.dot_general(q, kb_ref[pl.ds(start, CHUNK), :], (((1,), (1,)), ((), ())),
                                preferred_element_type=F32) * scale
            sp, cs = log_survival(z, False)
            w = weights(z, sp, cs[:, :CHUNK] + carry, False)
            acc = acc + jnp.dot(w, vb_ref[pl.ds(start, CHUNK), :], preferred_element_type=F32)
            return c - 1, carry + cs[:, CHUNK:], acc

        return lax.while_loop(cond, body, (c0, carry, acc))[2]

    def store(i, acc):
        start = i * CHUNK if isinstance(i, int) else pl.multiple_of(i * CHUNK, CHUNK)
        o_ref[0, pl.ds(start, CHUNK), :] = acc.astype(o_ref.dtype)

    first = list(range(WINDOW))
    for i, (_, _, acc) in zip(first, windows(first, [i + 1 for i in first])):
        store(i, acc)

    def group(g, _):
        blocks = [WINDOW + g * GROUP + b for b in range(GROUP)]
        done = windows(blocks, [WINDOW] * GROUP)
        worst = done[0][1]
        for _, carry, _ in done[1:]:
            worst = jnp.maximum(worst, carry)
        for i, (_, _, acc) in zip(blocks, done):
            store(i, acc)

        @pl.when(jnp.max(worst) > -EXP_ZERO)
        def _():
            for i, (q, carry, acc) in zip(blocks, done):
                store(i, far_chunks(q, i - WINDOW, carry, acc))

        return 0

    lax.fori_loop(0, (n_blocks - WINDOW) // GROUP, group, 0)


def _attention(proj3, t_pad):
    b = proj3.shape[0]
    n_blocks = t_pad // CHUNK
    assert n_blocks >= WINDOW and (n_blocks - WINDOW) % GROUP == 0
    blk = (1, t_pad, HEAD)
    return pl.pallas_call(
        functools.partial(_attn_kernel, n_blocks=n_blocks),
        grid=(b, N_HEADS),
        in_specs=[pl.BlockSpec(blk, lambda bi, h: (bi, 0, h)),
                  pl.BlockSpec(blk, lambda bi, h: (bi, 0, N_HEADS + h)),
                  pl.BlockSpec(blk, lambda bi, h: (bi, 0, 2 * N_HEADS + h))],
        out_specs=pl.BlockSpec(blk, lambda bi, h: (bi, 0, h)),
        out_shape=jax.ShapeDtypeStruct((b, t_pad, ATTN_WIDTH), F32),
        scratch_shapes=[pltpu.VMEM((t_pad, HEAD), BF16),
                        pltpu.VMEM((t_pad, HEAD), BF16)],
        compiler_params=_cparams(("arbitrary", "arbitrary")),
        name="stickbreak_attn",
    )(proj3, proj3, proj3)


def _shift_rows(x, d, row, fill):
    return jnp.where(row >= d, pltpu.roll(x, d, 0), fill)


def _lru_kernel(xr_ref, yg_ref, cw_ref, cb_ref, wa_ref, ba_ref, wi_ref, bi_ref,
                lam_ref, o_ref, a_ref, u_ref, g_ref, *, t_pad):
    x = xr_ref[0]
    row = lax.broadcasted_iota(jnp.int32, (t_pad, HEAD), 0)
    cw = cw_ref[...]
    xc = x * cw[3:4, :] + cb_ref[...]
    for tap in range(3):
        xc = xc + _shift_rows(x, 3 - tap, row, 0.0) * cw[tap:tap + 1, :]

    xcb = xc.astype(BF16)
    r = jax.nn.sigmoid(jnp.dot(xcb, wa_ref[...].astype(BF16), preferred_element_type=F32)
                       + ba_ref[...])
    gi = jax.nn.sigmoid(jnp.dot(xcb, wi_ref[...].astype(BF16), preferred_element_type=F32)
                        + bi_ref[...])
    log_a = (-LRU_C) * r * _softplus(-lam_ref[...])
    a = jnp.exp(log_a)
    two_log_a = 2.0 * log_a
    series = -two_log_a * (1.0 + two_log_a * (0.5 + two_log_a * (1.0 / 6.0 + two_log_a * (1.0 / 24.0))))
    one_minus = jnp.where(two_log_a > -0.05, series, 1.0 - a * a)
    u = jnp.sqrt(one_minus) * (gi * xc)

    in_group = row & 7
    for d in (1, 2, 4):
        keep = in_group >= d
        u = a * jnp.where(keep, pltpu.roll(u, d, 0), 0.0) + u
        a = a * jnp.where(keep, pltpu.roll(a, d, 0), 1.0)
    a_ref[...] = a
    u_ref[...] = u
    groups = t_pad // 8
    ga = a_ref[pl.ds(7, groups, stride=8), :]
    gu = u_ref[pl.ds(7, groups, stride=8), :]
    grow = lax.broadcasted_iota(jnp.int32, (groups, HEAD), 0)
    d = 1
    while d < groups:
        gu = ga * _shift_rows(gu, d, grow, 0.0) + gu
        if 2 * d < groups:
            ga = ga * _shift_rows(ga, d, grow, 1.0)
        d *= 2
    g_ref[...] = _shift_rows(gu, 1, grow, 0.0)

    def finish(g, _):
        rows = pl.ds(pl.multiple_of(g * 8, 8), 8)
        h_in = g_ref[pl.ds(g, 1), :]
        h = a_ref[rows, :] * h_in + u_ref[rows, :]
        o_ref[0, rows, :] = h * jax.nn.gelu(yg_ref[0, rows, :])
        return 0

    lax.fori_loop(0, groups, finish, 0, unroll=8)


def _lru_branch(proj3, cw, cb, wa, ba, wi, bi, lam, l, t_pad):
    b = proj3.shape[0]
    blk = (1, t_pad, HEAD)
    xr_off = 3 * ATTN_WIDTH // HEAD
    yg_off = xr_off + LRU_WIDTH // HEAD
    vspec = pl.BlockSpec((None, 1, HEAD), lambda bi_, h: (l, 0, h))
    mspec = pl.BlockSpec((None, None, HEAD, HEAD), lambda bi_, h: (l, h, 0, 0))
    return pl.pallas_call(
        functools.partial(_lru_kernel, t_pad=t_pad),
        grid=(b, N_HEADS),
        in_specs=[pl.BlockSpec(blk, lambda bi_, h: (bi_, 0, xr_off + h)),
                  pl.BlockSpec(blk, lambda bi_, h: (bi_, 0, yg_off + h)),
                  pl.BlockSpec((None, 4, HEAD), lambda bi_, h: (l, 0, h)),
                  vspec, mspec, vspec, mspec, vspec, vspec],
        out_specs=pl.BlockSpec(blk, lambda bi_, h: (bi_, 0, h)),
        out_shape=jax.ShapeDtypeStruct((b, t_pad, LRU_WIDTH), F32),
        scratch_shapes=[pltpu.VMEM((t_pad, HEAD), F32),
                        pltpu.VMEM((t_pad, HEAD), F32),
                        pltpu.VMEM((t_pad // 8, HEAD), F32)],
        compiler_params=_cparams(("arbitrary", "arbitrary")),
        name="rglru",
    )(proj3, proj3, cw, cb, wa, ba, wi, bi, lam)


def _rms_rows(x, g):
    ms = jnp.mean(x * x, axis=-1, keepdims=True)
    return x * lax.rsqrt(ms + RMS_EPS) * g


def _outproj_kernel(oa_ref, ol_ref, ga_ref, gl_ref, w_ref, h_ref, g_ref, b_ref, o_ref, ob_ref):
    na = _rms_rows(oa_ref[...], ga_ref[...]).astype(BF16)
    nl = _rms_rows(ol_ref[...], gl_ref[...]).astype(BF16)
    mix = (jnp.dot(na, w_ref[:ATTN_WIDTH, :], preferred_element_type=F32)
           + jnp.dot(nl, w_ref[ATTN_WIDTH:, :], preferred_element_type=F32))
    y = _layer_norm_rows(DEEPNORM_ALPHA * h_ref[...] + mix, g_ref[...], b_ref[...])
    o_ref[...] = y
    ob_ref[...] = y.astype(BF16)


def _outproj(oa, ol, ga, gl, w, l, h, g, b, bm):
    n, d = h.shape
    row = lambda wdt: pl.BlockSpec((bm, wdt), lambda i: (i, 0))
    vec = lambda wdt: pl.BlockSpec((None, 1, wdt), lambda i: (l, 0, 0))
    return pl.pallas_call(
        _outproj_kernel,
        grid=(n // bm,),
        in_specs=[row(ATTN_WIDTH), row(LRU_WIDTH), vec(ATTN_WIDTH), vec(LRU_WIDTH),
                  pl.BlockSpec((None, d, d), lambda i: (l, 0, 0)), row(d), vec(d), vec(d)],
        out_specs=(row(d), row(d)),
        out_shape=_stream_shapes(n, d),
        compiler_params=_cparams(("arbitrary",)),
        name="outproj_ln1",
    )(oa, ol, ga, gl, w, h, g, b)


def _ffn_conv(u, prev8, w_ref, b_ref, row8):
    w = w_ref[...]
    w0, w1, w2 = w[0:1, :], w[1:2, :], w[2:3, :]
    bias = b_ref[...]
    full = u * w2 + pltpu.roll(u, 1, 0) * w1 + pltpu.roll(u, 2, 0) * w0 + bias
    top = u[0:8, :]
    s1 = jnp.where(row8 >= 1, pltpu.roll(top, 1, 0), pltpu.roll(prev8, 1, 0))
    s2 = jnp.where(row8 >= 2, pltpu.roll(top, 2, 0), pltpu.roll(prev8, 2, 0))
    return full, top * w2 + s1 * w1 + s2 * w0 + bias


def _ffn_up_kernel(x_ref, wg_ref, wv_ref, cwg_ref, cwv_ref, cbg_ref, cbv_ref, o_ref,
                   ug_ref, uv_ref, hg_ref, hv_ref, *, nck, tiles_per_seq):
    s = pl.program_id(0)

    @pl.when(s == 0)
    def _():
        ug_ref[...] = jnp.zeros_like(ug_ref)
        uv_ref[...] = jnp.zeros_like(uv_ref)
        hg_ref[...] = jnp.zeros_like(hg_ref)
        hv_ref[...] = jnp.zeros_like(hv_ref)

    prev = jnp.maximum(s - 1, 0)
    ip = prev // nck
    jp = prev % nck
    ug = ug_ref[...]
    uv = uv_ref[...]
    bm, ck = ug.shape
    seq_start = (ip % tiles_per_seq) == 0
    prev_g = jnp.where(seq_start, 0.0, hg_ref[jp])
    prev_v = jnp.where(seq_start, 0.0, hv_ref[jp])
    hg_ref[jp] = ug[bm - 8:, :]
    hv_ref[jp] = uv[bm - 8:, :]
    row8 = lax.broadcasted_iota(jnp.int32, (8, ck), 0)
    cg, cg_top = _ffn_conv(ug, prev_g, cwg_ref, cbg_ref, row8)
    cv, cv_top = _ffn_conv(uv, prev_v, cwv_ref, cbv_ref, row8)
    o_ref[...] = (jax.nn.silu(cg) * cv).astype(o_ref.dtype)
    o_ref[0:8, :] = (jax.nn.silu(cg_top) * cv_top).astype(o_ref.dtype)

    x = x_ref[...]
    ug_ref[...] = jnp.dot(x, wg_ref[...], preferred_element_type=F32)
    uv_ref[...] = jnp.dot(x, wv_ref[...], preferred_element_type=F32)


def _ffn_up(hb, w_up, conv_w, conv_b, l, bm, ck, tiles_per_seq):
    n, d = hb.shape
    nck = FFN_HIDDEN // ck
    steps = (n // bm) * nck
    cur = lambda s: jnp.minimum(s, steps - 1)
    prev = lambda s: jnp.maximum(s - 1, 0)
    return pl.pallas_call(
        functools.partial(_ffn_up_kernel, nck=nck, tiles_per_seq=tiles_per_seq),
        grid=(steps + 1,),
        in_specs=[pl.BlockSpec((bm, d), lambda s: (cur(s) // nck, 0)),
                  pl.BlockSpec((None, d, ck), lambda s: (l, 0, cur(s) % nck)),
                  pl.BlockSpec((None, d, ck), lambda s: (l, 0, nck + cur(s) % nck)),
                  pl.BlockSpec((None, 3, ck), lambda s: (l, 0, prev(s) % nck)),
                  pl.BlockSpec((None, 3, ck), lambda s: (l, 0, nck + prev(s) % nck)),
                  pl.BlockSpec((None, 1, ck), lambda s: (l, 0, prev(s) % nck)),
                  pl.BlockSpec((None, 1, ck), lambda s: (l, 0, nck + prev(s) % nck))],
        out_specs=pl.BlockSpec((bm, ck), lambda s: (prev(s) // nck, prev(s) % nck)),
        out_shape=jax.ShapeDtypeStruct((n, FFN_HIDDEN), BF16),
        scratch_shapes=[pltpu.VMEM((bm, ck), F32),
                        pltpu.VMEM((bm, ck), F32),
                        pltpu.VMEM((nck, 8, ck), F32),
                        pltpu.VMEM((nck, 8, ck), F32)],
        compiler_params=_cparams(("arbitrary",)),
        name="ffn_up_conv_glu",
    )(hb, w_up, w_up, conv_w, conv_w, conv_b, conv_b)


def _ffn_down_kernel(a_ref, w_ref, h_ref, g_ref, b_ref, o_ref, ob_ref):
    o_ref[...] = jnp.dot(a_ref[...], w_ref[...], preferred_element_type=F32)
    for r in range(0, o_ref.shape[0], LN_SLAB):
        rows = pl.ds(r, LN_SLAB)
        y = _layer_norm_rows(DEEPNORM_ALPHA * h_ref[rows, :] + o_ref[rows, :],
                             g_ref[...], b_ref[...])
        o_ref[rows, :] = y
        ob_ref[rows, :] = y.astype(BF16)


def _ffn_down(act, w, l, h, g, b, bm):
    n, d = h.shape
    assert bm % LN_SLAB == 0 and n % bm == 0
    row = pl.BlockSpec((bm, d), lambda i: (i, 0))
    vec = pl.BlockSpec((None, 1, d), lambda i: (l, 0, 0))
    return pl.pallas_call(
        _ffn_down_kernel,
        grid=(n // bm,),
        in_specs=[pl.BlockSpec((bm, FFN_HIDDEN), lambda i: (i, 0)),
                  pl.BlockSpec((None, FFN_HIDDEN, d), lambda i: (l, 0, 0),
                               pipeline_mode=pl.Buffered(1)),
                  row, vec, vec],
        out_specs=(row, row),
        out_shape=_stream_shapes(n, d),
        compiler_params=_cparams(("arbitrary",)),
        name="ffn_down_ln2",
    )(act, w, h, g, b)


def kernel(x, meta_tokens, ln0_g, ln0_b, w_in, lru_conv_w, lru_conv_b, lru_w_a, lru_b_a, lru_w_i, lru_b_i, lru_lambda, g_attn, g_lru, w_out, ln1_g, ln1_b, ffn_w_up, ffn_conv_w, ffn_conv_b, ffn_w_down, ln2_g, ln2_b):
    b, seq, d = x.shape
    t = N_META + seq
    t_pad = -(-t // CHUNK) * CHUNK
    tiles_per_seq = 4
    bm = t_pad // tiles_per_seq
    n = b * t_pad

    meta = jnp.broadcast_to(meta_tokens[None].astype(x.dtype), (b, N_META, d))
    tail = jnp.zeros((b, t_pad - t, d), x.dtype)
    h = jnp.concatenate([meta, x, tail], axis=1).reshape(n, d)
    h, hb = _ln_rows(h, ln0_g, ln0_b, bm // 2)

    vec = lambda a: a.reshape(DEPTH, 1, -1)
    w_in_b, w_out_b = w_in.astype(BF16), w_out.astype(BF16)
    w_up_b, w_down_b = ffn_w_up.astype(BF16), ffn_w_down.astype(BF16)
    lru_vecs = [vec(a) for a in (lru_conv_b, lru_b_a, lru_b_i, lru_lambda)]
    g_attn, g_lru, ln1_g, ln1_b, ln2_g, ln2_b, ffn_conv_b = (
        vec(a) for a in (g_attn, g_lru, ln1_g, ln1_b, ln2_g, ln2_b, ffn_conv_b))

    for l in range(DEPTH):
        proj = _inproj(hb, w_in_b, l, bm, 1024)
        proj3 = proj.reshape(b, t_pad, IN_COLS)
        o_attn = _attention(proj3, t_pad)
        o_lru = _lru_branch(proj3, lru_conv_w, lru_vecs[0], lru_w_a, lru_vecs[1],
                            lru_w_i, lru_vecs[2], lru_vecs[3], l, t_pad)
        h, hb = _outproj(o_attn.reshape(n, ATTN_WIDTH), o_lru.reshape(n, LRU_WIDTH),
                         g_attn, g_lru, w_out_b, l, h, ln1_g, ln1_b, bm // 2)
        act = _ffn_up(hb, w_up_b, ffn_conv_w, ffn_conv_b, l, bm, 512, tiles_per_seq)
        h, hb = _ffn_down(act, w_down_b, l, h, ln2_g, ln2_b, bm // 3)

    return h.reshape(b, t_pad, d)[:, N_META:t]
```

```python
import functools

import jax
import jax.numpy as jnp
from jax import lax
from jax.experimental import pallas as pl
from jax.experimental.pallas import tpu as pltpu

F32 = jnp.float32
BF16 = jnp.bfloat16

D_MODEL = 2048
DEPTH = 4
N_META = 16
HEAD = 128
ATTN_WIDTH = 1024
LRU_WIDTH = 1024
N_HEADS = 8
IN_COLS = 3 * ATTN_WIDTH + 2 * LRU_WIDTH
FFN_HIDDEN = 5632
LRU_C = 8.0
DEEPNORM_ALPHA = (2 * DEPTH) ** 0.25
LN_EPS = 1e-5
RMS_EPS = 1e-6

CHUNK = 128
EXP_ZERO = 104.0
WINDOW = 3
GROUP = 6
LN_SLAB = 32

VMEM_LIMIT = 56 * 1024 * 1024


def _cparams(sem):
    return pltpu.CompilerParams(dimension_semantics=sem, vmem_limit_bytes=VMEM_LIMIT)


def _layer_norm_rows(y, g, b):
    mu = jnp.mean(y, axis=-1, keepdims=True)
    yc = y - mu
    var = jnp.mean(yc * yc, axis=-1, keepdims=True)
    return yc * lax.rsqrt(var + LN_EPS) * g + b


def _softplus(x):
    return jnp.maximum(x, 0.0) + jnp.log(1.0 + jnp.exp(-jnp.abs(x)))


def _stream_shapes(n, d):
    return (jax.ShapeDtypeStruct((n, d), F32), jax.ShapeDtypeStruct((n, d), BF16))


def _ln_kernel(x_ref, g_ref, b_ref, o_ref, ob_ref):
    y = _layer_norm_rows(x_ref[...], g_ref[...], b_ref[...])
    o_ref[...] = y
    ob_ref[...] = y.astype(BF16)


def _ln_rows(x, g, b, bm):
    n, d = x.shape
    row = pl.BlockSpec((bm, d), lambda i: (i, 0))
    vec = pl.BlockSpec((1, d), lambda i: (0, 0))
    return pl.pallas_call(
        _ln_kernel,
        grid=(n // bm,),
        in_specs=[row, vec, vec],
        out_specs=(row, row),
        out_shape=_stream_shapes(n, d),
        compiler_params=_cparams(("arbitrary",)),
        name="ln0",
    )(x, g.reshape(1, d), b.reshape(1, d))


def _inproj_kernel(h_ref, w_ref, o_ref):
    o_ref[...] = jnp.dot(h_ref[...], w_ref[...], preferred_element_type=F32)


def _inproj(hb, w, l, bm, bn):
    n, d = hb.shape
    cols = w.shape[-1]
    return pl.pallas_call(
        _inproj_kernel,
        grid=(n // bm, cols // bn),
        in_specs=[pl.BlockSpec((bm, d), lambda i, j: (i, 0)),
                  pl.BlockSpec((None, d, bn), lambda i, j: (l, 0, j))],
        out_specs=pl.BlockSpec((bm, bn), lambda i, j: (i, j)),
        out_shape=jax.ShapeDtypeStruct((n, cols), F32),
        compiler_params=_cparams(("arbitrary", "arbitrary")),
        name="inproj",
    )(hb, w)


def _attn_kernel(q_ref, k_ref, v_ref, o_ref, kb_ref, vb_ref, *, n_blocks):
    kb_ref[...] = k_ref[0].astype(BF16)
    vb_ref[...] = v_ref[0].astype(BF16)
    scale = HEAD ** -0.5

    row = lax.broadcasted_iota(jnp.int32, (CHUNK, CHUNK), 0)
    col = lax.broadcasted_iota(jnp.int32, (CHUNK, CHUNK), 1)
    strictly_earlier = col < row
    jj = lax.broadcasted_iota(jnp.int32, (2 * CHUNK, 2 * CHUNK), 0)
    ss = lax.broadcasted_iota(jnp.int32, (2 * CHUNK, 2 * CHUNK), 1)
    suffix_ones = jnp.where((ss >= CHUNK) | ((jj & (CHUNK - 1)) > ss), 1.0, 0.0).astype(BF16)

    def log_survival(z, diagonal):
        sp = _softplus(z)
        l = -sp
        if diagonal:
            l = jnp.where(strictly_earlier, l, 0.0)
        l_hi = l.astype(BF16)
        l_lo = (l - l_hi.astype(F32)).astype(BF16)
        cs = jnp.dot(jnp.concatenate([l_hi, l_lo], axis=1), suffix_ones,
                     preferred_element_type=F32)
        return sp, cs

    def weights(z, sp, later, diagonal):
        w = jnp.exp(z - sp + later)
        if diagonal:
            w = jnp.where(strictly_earlier, w, 0.0)
        return w.astype(BF16)

    def windows(blocks, nwins):
        qs, vws, zs = [], [], []
        for i, nwin in zip(blocks, nwins):
            qstart = i * CHUNK
            kstart = (i - nwin + 1) * CHUNK
            if not isinstance(i, int):
                qstart = pl.multiple_of(qstart, CHUNK)
                kstart = pl.multiple_of(kstart, CHUNK)
            q = q_ref[0, pl.ds(qstart, CHUNK), :].astype(BF16)
            kw = kb_ref[pl.ds(kstart, nwin * CHUNK), :]
            vws.append(vb_ref[pl.ds(kstart, nwin * CHUNK), :])
            z = lax.dot_general(q, kw, (((1,), (1,)), ((), ())),
                                preferred_element_type=F32) * scale
            qs.append(q)
            zs.append([z[:, c * CHUNK:(c + 1) * CHUNK] for c in range(nwin)])
        stats = [[log_survival(zc, c == len(z) - 1) for c, zc in enumerate(z)] for z in zs]
        carries, ws = [], []
        for z, stat in zip(zs, stats):
            nwin = len(z)
            carry = None
            w = [None] * nwin
            for c in reversed(range(nwin)):
                sp, cs = stat[c]
                later = cs[:, :CHUNK] if carry is None else cs[:, :CHUNK] + carry
                w[c] = weights(z[c], sp, later, c == nwin - 1)
                carry = cs[:, CHUNK:] if carry is None else carry + cs[:, CHUNK:]
            carries.append(carry)
            ws.append(w[0] if nwin == 1 else jnp.concatenate(w, axis=1))
        accs = [jnp.dot(w, vw, preferred_element_type=F32) for w, vw in zip(ws, vws)]
        return list(zip(qs, carries, accs))

    def far_chunks(q, c0, carry, acc):
        def cond(s):
            c, carry, _ = s
            return jnp.logical_and(c >= 0, jnp.max(carry) > -EXP_ZERO)

        def body(s):
            c, carry, acc = s
            start = pl.multiple_of(c * CHUNK, CHUNK)
            z = lax.dot_general(q, kb_ref[pl.ds(start, CHUNK), :], (((1,), (1,)), ((), ())),
                                preferred_element_type=F32) * scale
            sp, cs = log_survival(z, False)
            w = weights(z, sp, cs[:, :CHUNK] + carry, False)
            acc = acc + jnp.dot(w, vb_ref[pl.ds(start, CHUNK), :], preferred_element_type=F32)
            return c - 1, carry + cs[:, CHUNK:], acc

        return lax.while_loop(cond, body, (c0, carry, acc))[2]

    def store(i, acc):
        start = i * CHUNK if isinstance(i, int) else pl.multiple_of(i * CHUNK, CHUNK)
        o_ref[0, pl.ds(start, CHUNK), :] = acc.astype(o_ref.dtype)

    first = list(range(WINDOW))
    for i, (_, _, acc) in zip(first, windows(first, [i + 1 for i in first])):
        store(i, acc)

    def group(g, _):
        blocks = [WINDOW + g * GROUP + b for b in range(GROUP)]
        done = windows(blocks, [WINDOW] * GROUP)
        worst = done[0][1]
        for _, carry, _ in done[1:]:
            worst = jnp.maximum(worst, carry)
        for i, (_, _, acc) in zip(blocks, done):
            store(i, acc)

        @pl.when(jnp.max(worst) > -EXP_ZERO)
        def _():
            for i, (q, carry, acc) in zip(blocks, done):
                store(i, far_chunks(q, i - WINDOW, carry, acc))

        return 0

    lax.fori_loop(0, (n_blocks - WINDOW) // GROUP, group, 0)


def _attention(proj3, t_pad):
    b = proj3.shape[0]
    n_blocks = t_pad // CHUNK
    assert n_blocks >= WINDOW and (n_blocks - WINDOW) % GROUP == 0
    blk = (1, t_pad, HEAD)
    return pl.pallas_call(
        functools.partial(_attn_kernel, n_blocks=n_blocks),
        grid=(b, N_HEADS),
        in_specs=[pl.BlockSpec(blk, lambda bi, h: (bi, 0, h)),
                  pl.BlockSpec(blk, lambda bi, h: (bi, 0, N_HEADS + h)),
                  pl.BlockSpec(blk, lambda bi, h: (bi, 0, 2 * N_HEADS + h))],
        out_specs=pl.BlockSpec(blk, lambda bi, h: (bi, 0, h)),
        out_shape=jax.ShapeDtypeStruct((b, t_pad, ATTN_WIDTH), F32),
        scratch_shapes=[pltpu.VMEM((t_pad, HEAD), BF16),
                        pltpu.VMEM((t_pad, HEAD), BF16)],
        compiler_params=_cparams(("arbitrary", "arbitrary")),
        name="stickbreak_attn",
    )(proj3, proj3, proj3)


def _lru_kernel(xr_ref, yg_ref, cw_ref, cb_ref, wa_ref, ba_ref, wi_ref, bi_ref,
                lam_ref, o_ref, x_ref, xc_ref, a_ref, u_ref, *, t_pad, groups):
    rows_p = 8 * groups
    x_ref[0:t_pad, :] = xr_ref[0]
    x_ref[t_pad:rows_p, :] = jnp.zeros((rows_p - t_pad, HEAD), F32)
    sub = lax.broadcasted_iota(jnp.int32, (8, HEAD), 0)
    load = lambda g: x_ref[pl.ds(g, 8, stride=groups), :]
    group = lambda g: slice(8 * g, 8 * g + 8)

    def chunk_down(v, fill):
        return jnp.where(sub >= 1, pltpu.roll(v, 1, 0), fill)

    w0, w1, w2, w3, bias = (jnp.broadcast_to(r, (8, HEAD)) for r in
                            (cw_ref[0:1, :], cw_ref[1:2, :], cw_ref[2:3, :], cw_ref[3:4, :],
                             cb_ref[...]))
    p3, p2, p1 = (chunk_down(load(groups - k), 0.0) for k in (3, 2, 1))
    for g in range(groups):
        cur = load(g)
        xc_ref[group(g), :] = cur * w3 + p1 * w2 + p2 * w1 + p3 * w0 + bias
        p3, p2, p1 = p2, p1, cur

    xc = xc_ref[...]
    xcb = xc.astype(BF16)
    r = jax.nn.sigmoid(jnp.dot(xcb, wa_ref[...].astype(BF16), preferred_element_type=F32)
                       + ba_ref[...])
    gi = jax.nn.sigmoid(jnp.dot(xcb, wi_ref[...].astype(BF16), preferred_element_type=F32)
                        + bi_ref[...])
    log_a = (-LRU_C) * r * _softplus(-lam_ref[...])
    a_ref[...] = jnp.exp(log_a)
    th = jnp.tanh(log_a)
    u_ref[...] = jnp.sqrt(-2.0 * th / (1.0 - th)) * (gi * xc)

    h = jnp.zeros((8, HEAD), F32)
    p = jnp.ones((8, HEAD), F32)
    for g in range(groups):
        a_g = a_ref[group(g), :]
        p = a_g * p
        h = a_g * h + u_ref[group(g), :]
        a_ref[group(g), :] = p
        u_ref[group(g), :] = h

    for d in (1, 2, 4):
        keep = sub >= d
        h = p * jnp.where(keep, pltpu.roll(h, d, 0), 0.0) + h
        p = p * jnp.where(keep, pltpu.roll(p, d, 0), 1.0)
    h_in = chunk_down(h, 0.0)
    for g in range(groups):
        x_ref[pl.ds(g, 8, stride=groups), :] = u_ref[group(g), :] + a_ref[group(g), :] * h_in

    o_ref[0] = x_ref[0:t_pad, :] * jax.nn.gelu(yg_ref[0])


def _lru_branch(proj3, cw, cb, wa, ba, wi, bi, lam, l, t_pad):
    b = proj3.shape[0]
    blk = (1, t_pad, HEAD)
    groups = t_pad // 8 + 1
    groups += 1 - groups % 2
    xr_off = 3 * ATTN_WIDTH // HEAD
    yg_off = xr_off + LRU_WIDTH // HEAD
    vspec = pl.BlockSpec((None, 1, HEAD), lambda bi_, h: (l, 0, h))
    mspec = pl.BlockSpec((None, None, HEAD, HEAD), lambda bi_, h: (l, h, 0, 0))
    return pl.pallas_call(
        functools.partial(_lru_kernel, t_pad=t_pad, groups=groups),
        grid=(b, N_HEADS),
        in_specs=[pl.BlockSpec(blk, lambda bi_, h: (bi_, 0, xr_off + h)),
                  pl.BlockSpec(blk, lambda bi_, h: (bi_, 0, yg_off + h)),
                  pl.BlockSpec((None, 4, HEAD), lambda bi_, h: (l, 0, h)),
                  vspec, mspec, vspec, mspec, vspec, vspec],
        out_specs=pl.BlockSpec(blk, lambda bi_, h: (bi_, 0, h)),
        out_shape=jax.ShapeDtypeStruct((b, t_pad, LRU_WIDTH), F32),
        scratch_shapes=[pltpu.VMEM((8 * groups, HEAD), F32)] * 4,
        compiler_params=_cparams(("arbitrary", "arbitrary")),
        name="rglru",
    )(proj3, proj3, cw, cb, wa, ba, wi, bi, lam)


def _rms_rows(x, g):
    ms = jnp.mean(x * x, axis=-1, keepdims=True)
    return x * lax.rsqrt(ms + RMS_EPS) * g


def _outproj_kernel(oa_ref, ol_ref, ga_ref, gl_ref, w_ref, h_ref, g_ref, b_ref, o_ref, ob_ref):
    na = _rms_rows(oa_ref[...], ga_ref[...]).astype(BF16)
    nl = _rms_rows(ol_ref[...], gl_ref[...]).astype(BF16)
    mix = (jnp.dot(na, w_ref[:ATTN_WIDTH, :], preferred_element_type=F32)
           + jnp.dot(nl, w_ref[ATTN_WIDTH:, :], preferred_element_type=F32))
    y = _layer_norm_rows(DEEPNORM_ALPHA * h_ref[...] + mix, g_ref[...], b_ref[...])
    o_ref[...] = y
    ob_ref[...] = y.astype(BF16)


def _outproj(oa, ol, ga, gl, w, l, h, g, b, bm):
    n, d = h.shape
    row = lambda wdt: pl.BlockSpec((bm, wdt), lambda i: (i, 0))
    vec = lambda wdt: pl.BlockSpec((None, 1, wdt), lambda i: (l, 0, 0))
    return pl.pallas_call(
        _outproj_kernel,
        grid=(n // bm,),
        in_specs=[row(ATTN_WIDTH), row(LRU_WIDTH), vec(ATTN_WIDTH), vec(LRU_WIDTH),
                  pl.BlockSpec((None, d, d), lambda i: (l, 0, 0)), row(d), vec(d), vec(d)],
        out_specs=(row(d), row(d)),
        out_shape=_stream_shapes(n, d),
        compiler_params=_cparams(("arbitrary",)),
        name="outproj_ln1",
    )(oa, ol, ga, gl, w, h, g, b)


def _ffn_up_kernel(x_ref, wg_ref, wv_ref, cwg_ref, cwv_ref, cbg_ref, cbv_ref, o_ref,
                   ug_ref, uv_ref, hg_ref, hv_ref, nat_ref, *, nck, tiles_per_seq):
    s = pl.program_id(0)
    slabs, bm, lanes = ug_ref.shape
    groups = bm // 8

    @pl.when(s == 0)
    def _():
        ug_ref[...] = jnp.zeros_like(ug_ref)
        uv_ref[...] = jnp.zeros_like(uv_ref)
        hg_ref[...] = jnp.zeros_like(hg_ref)
        hv_ref[...] = jnp.zeros_like(hv_ref)

    prev = jnp.maximum(s - 1, 0)
    ip = prev // nck
    jp = prev % nck
    seq_start = (ip % tiles_per_seq) == 0
    sub = lax.broadcasted_iota(jnp.int32, (8, lanes), 0)

    def conv_taps(u_ref, tail_ref, w_ref, b_ref, j):
        cols = slice(j * lanes, (j + 1) * lanes)
        w0, w1, w2, bias = (jnp.broadcast_to(r, (8, lanes)) for r in
                            (w_ref[0:1, cols], w_ref[1:2, cols], w_ref[2:3, cols], b_ref[:, cols]))
        load = lambda g: u_ref[j, pl.ds(g, 8, stride=groups), :]
        last2, last1 = load(groups - 2), load(groups - 1)
        tail2 = jnp.where(seq_start, 0.0, tail_ref[jp, j, 0:8, :])
        tail1 = jnp.where(seq_start, 0.0, tail_ref[jp, j, 8:16, :])
        tail_ref[jp, j, 0:8, :] = last2
        tail_ref[jp, j, 8:16, :] = last1
        before2 = jnp.where(sub >= 1, pltpu.roll(last2, 1, 0), pltpu.roll(tail2, 1, 0))
        before1 = jnp.where(sub >= 1, pltpu.roll(last1, 1, 0), pltpu.roll(tail1, 1, 0))
        conv = lambda cur, b1, b2: cur * w2 + b1 * w1 + b2 * w0 + bias
        return load, conv, before1, before2

    for j in range(slabs):
        load_g, conv_g, g1, g2 = conv_taps(ug_ref, hg_ref, cwg_ref, cbg_ref, j)
        load_v, conv_v, v1, v2 = conv_taps(uv_ref, hv_ref, cwv_ref, cbv_ref, j)
        for g in range(groups):
            cur_g, cur_v = load_g(g), load_v(g)
            act = jax.nn.silu(conv_g(cur_g, g1, g2)) * conv_v(cur_v, v1, v2)
            nat_ref[j, pl.ds(g, 8, stride=groups), :] = act
            g2, g1, v2, v1 = g1, cur_g, v1, cur_v
        o_ref[:, j * lanes:(j + 1) * lanes] = nat_ref[j].astype(o_ref.dtype)

    x = x_ref[...]
    dg = jnp.dot(x, wg_ref[...], preferred_element_type=F32)
    dv = jnp.dot(x, wv_ref[...], preferred_element_type=F32)
    for j in range(slabs):
        ug_ref[j] = dg[:, j * lanes:(j + 1) * lanes]
        uv_ref[j] = dv[:, j * lanes:(j + 1) * lanes]


def _ffn_up(hb, w_up, conv_w, conv_b, l, bm, ck, tiles_per_seq):
    n, d = hb.shape
    nck = FFN_HIDDEN // ck
    slabs = ck // 128
    steps = (n // bm) * nck
    cur = lambda s: jnp.minimum(s, steps - 1)
    prev = lambda s: jnp.maximum(s - 1, 0)
    return pl.pallas_call(
        functools.partial(_ffn_up_kernel, nck=nck, tiles_per_seq=tiles_per_seq),
        grid=(steps + 1,),
        in_specs=[pl.BlockSpec((bm, d), lambda s: (cur(s) // nck, 0)),
                  pl.BlockSpec((None, d, ck), lambda s: (l, 0, cur(s) % nck)),
                  pl.BlockSpec((None, d, ck), lambda s: (l, 0, nck + cur(s) % nck)),
                  pl.BlockSpec((None, 3, ck), lambda s: (l, 0, prev(s) % nck)),
                  pl.BlockSpec((None, 3, ck), lambda s: (l, 0, nck + prev(s) % nck)),
                  pl.BlockSpec((None, 1, ck), lambda s: (l, 0, prev(s) % nck)),
                  pl.BlockSpec((None, 1, ck), lambda s: (l, 0, nck + prev(s) % nck))],
        out_specs=pl.BlockSpec((bm, ck), lambda s: (prev(s) // nck, prev(s) % nck)),
        out_shape=jax.ShapeDtypeStruct((n, FFN_HIDDEN), BF16),
        scratch_shapes=[pltpu.VMEM((slabs, bm, 128), F32),
                        pltpu.VMEM((slabs, bm, 128), F32),
                        pltpu.VMEM((nck, slabs, 16, 128), F32),
                        pltpu.VMEM((nck, slabs, 16, 128), F32),
                        pltpu.VMEM((slabs, bm, 128), F32)],
        compiler_params=_cparams(("arbitrary",)),
        name="ffn_up_conv_glu",
    )(hb, w_up, w_up, conv_w, conv_w, conv_b, conv_b)


def _ffn_down_kernel(a_ref, w_ref, h_ref, g_ref, b_ref, o_ref, ob_ref):
    o_ref[...] = jnp.dot(a_ref[...], w_ref[...], preferred_element_type=F32)
    for r in range(0, o_ref.shape[0], LN_SLAB):
        rows = pl.ds(r, LN_SLAB)
        y = _layer_norm_rows(DEEPNORM_ALPHA * h_ref[rows, :] + o_ref[rows, :],
                             g_ref[...], b_ref[...])
        o_ref[rows, :] = y
        ob_ref[rows, :] = y.astype(BF16)


def _ffn_down(act, w, l, h, g, b, bm):
    n, d = h.shape
    assert bm % LN_SLAB == 0 and n % bm == 0
    row = pl.BlockSpec((bm, d), lambda i: (i, 0))
    vec = pl.BlockSpec((None, 1, d), lambda i: (l, 0, 0))
    return pl.pallas_call(
        _ffn_down_kernel,
        grid=(n // bm,),
        in_specs=[pl.BlockSpec((bm, FFN_HIDDEN), lambda i: (i, 0)),
                  pl.BlockSpec((None, FFN_HIDDEN, d), lambda i: (l, 0, 0),
                               pipeline_mode=pl.Buffered(1)),
                  row, vec, vec],
        out_specs=(row, row),
        out_shape=_stream_shapes(n, d),
        compiler_params=_cparams(("arbitrary",)),
        name="ffn_down_ln2",
    )(act, w, h, g, b)


def kernel(x, meta_tokens, ln0_g, ln0_b, w_in, lru_conv_w, lru_conv_b, lru_w_a, lru_b_a, lru_w_i, lru_b_i, lru_lambda, g_attn, g_lru, w_out, ln1_g, ln1_b, ffn_w_up, ffn_conv_w, ffn_conv_b, ffn_w_down, ln2_g, ln2_b):
    b, seq, d = x.shape
    t = N_META + seq
    t_pad = -(-t // CHUNK) * CHUNK
    tiles_per_seq = 4
    bm = t_pad // tiles_per_seq
    n = b * t_pad

    meta = jnp.broadcast_to(meta_tokens[None].astype(x.dtype), (b, N_META, d))
    tail = jnp.zeros((b, t_pad - t, d), x.dtype)
    h = jnp.concatenate([meta, x, tail], axis=1).reshape(n, d)
    h, hb = _ln_rows(h, ln0_g, ln0_b, bm // 2)

    vec = lambda a: a.reshape(DEPTH, 1, -1)
    w_in_b, w_out_b = w_in.astype(BF16), w_out.astype(BF16)
    w_up_b, w_down_b = ffn_w_up.astype(BF16), ffn_w_down.astype(BF16)
    lru_vecs = [vec(a) for a in (lru_conv_b, lru_b_a, lru_b_i, lru_lambda)]
    g_attn, g_lru, ln1_g, ln1_b, ln2_g, ln2_b, ffn_conv_b = (
        vec(a) for a in (g_attn, g_lru, ln1_g, ln1_b, ln2_g, ln2_b, ffn_conv_b))

    for l in range(DEPTH):
        proj = _inproj(hb, w_in_b, l, bm, 1024)
        proj3 = proj.reshape(b, t_pad, IN_COLS)
        o_attn = _attention(proj3, t_pad)
        o_lru = _lru_branch(proj3, lru_conv_w, lru_vecs[0], lru_w_a, lru_vecs[1],
                            lru_w_i, lru_vecs[2], lru_vecs[3], l, t_pad)
        h, hb = _outproj(o_attn.reshape(n, ATTN_WIDTH), o_lru.reshape(n, LRU_WIDTH),
                         g_attn, g_lru, w_out_b, l, h, ln1_g, ln1_b, bm // 2)
        act = _ffn_up(hb, w_up_b, ffn_conv_w, ffn_conv_b, l, bm, 512, tiles_per_seq)
        h, hb = _ffn_down(act, w_down_b, l, h, ln2_g, ln2_b, bm // 3)

    return h.reshape(b, t_pad, d)[:, N_META:t]
```

```python
import functools

import jax
import jax.numpy as jnp
from jax import lax
from jax.experimental import pallas as pl
from jax.experimental.pallas import tpu as pltpu

F32 = jnp.float32
BF16 = jnp.bfloat16

D_MODEL = 2048
DEPTH = 4
N_META = 16
HEAD = 128
ATTN_WIDTH = 1024
LRU_WIDTH = 1024
N_HEADS = 8
IN_COLS = 3 * ATTN_WIDTH + 2 * LRU_WIDTH
FFN_HIDDEN = 5632
LRU_C = 8.0
DEEPNORM_ALPHA = (2 * DEPTH) ** 0.25
LN_EPS = 1e-5
RMS_EPS = 1e-6

CHUNK = 128
EXP_ZERO = 104.0
WINDOW = 3
GROUP = 10
LN_SLAB = 32

VMEM_LIMIT = 56 * 1024 * 1024


def _cparams(sem):
    return pltpu.CompilerParams(dimension_semantics=sem, vmem_limit_bytes=VMEM_LIMIT)


def _layer_norm_rows(y, g, b):
    mu = jnp.mean(y, axis=-1, keepdims=True)
    yc = y - mu
    var = jnp.mean(yc * yc, axis=-1, keepdims=True)
    return yc * lax.rsqrt(var + LN_EPS) * g + b


def _softplus(x):
    return jnp.maximum(x, 0.0) + jnp.log(1.0 + jnp.exp(-jnp.abs(x)))


def _stream_shapes(n, d):
    return (jax.ShapeDtypeStruct((n, d), F32), jax.ShapeDtypeStruct((n, d), BF16))


def _ln_kernel(x_ref, g_ref, b_ref, o_ref, ob_ref):
    y = _layer_norm_rows(x_ref[...], g_ref[...], b_ref[...])
    o_ref[...] = y
    ob_ref[...] = y.astype(BF16)


def _ln_rows(x, g, b, bm):
    n, d = x.shape
    row = pl.BlockSpec((bm, d), lambda i: (i, 0))
    vec = pl.BlockSpec((1, d), lambda i: (0, 0))
    return pl.pallas_call(
        _ln_kernel,
        grid=(n // bm,),
        in_specs=[row, vec, vec],
        out_specs=(row, row),
        out_shape=_stream_shapes(n, d),
        compiler_params=_cparams(("arbitrary",)),
        name="ln0",
    )(x, g.reshape(1, d), b.reshape(1, d))


def _inproj_kernel(h_ref, w_ref, o_ref):
    o_ref[...] = jnp.dot(h_ref[...], w_ref[...], preferred_element_type=F32)


def _inproj(hb, w, l, bm, bn):
    n, d = hb.shape
    cols = w.shape[-1]
    return pl.pallas_call(
        _inproj_kernel,
        grid=(n // bm, cols // bn),
        in_specs=[pl.BlockSpec((bm, d), lambda i, j: (i, 0)),
                  pl.BlockSpec((None, d, bn), lambda i, j: (l, 0, j))],
        out_specs=pl.BlockSpec((bm, bn), lambda i, j: (i, j)),
        out_shape=jax.ShapeDtypeStruct((n, cols), F32),
        compiler_params=_cparams(("arbitrary", "arbitrary")),
        name="inproj",
    )(hb, w)


def _attn_kernel(q_ref, k_ref, v_ref, o_ref, kb_ref, vb_ref, *, n_blocks):
    kb_ref[...] = k_ref[0].astype(BF16)
    vb_ref[...] = v_ref[0].astype(BF16)
    scale = HEAD ** -0.5

    row = lax.broadcasted_iota(jnp.int32, (CHUNK, CHUNK), 0)
    col = lax.broadcasted_iota(jnp.int32, (CHUNK, CHUNK), 1)
    strictly_earlier = col < row
    jj = lax.broadcasted_iota(jnp.int32, (2 * CHUNK, 2 * CHUNK), 0)
    ss = lax.broadcasted_iota(jnp.int32, (2 * CHUNK, 2 * CHUNK), 1)
    minus_suffix = jnp.where((ss >= CHUNK) | ((jj & (CHUNK - 1)) > ss), -1.0, 0.0).astype(BF16)

    def log_survival(z, diagonal):
        sp = _softplus(z)
        m = jnp.where(strictly_earlier, sp, 0.0) if diagonal else sp
        m_hi = m.astype(BF16)
        m_lo = (m - m_hi.astype(F32)).astype(BF16)
        cs = jnp.dot(jnp.concatenate([m_hi, m_lo], axis=1), minus_suffix,
                     preferred_element_type=F32)
        return sp, cs

    def weights(z, sp, later, diagonal):
        w = jnp.exp(z - sp + later)
        if diagonal:
            w = jnp.where(strictly_earlier, w, 0.0)
        return w.astype(BF16)

    def windows(blocks, nwins):
        qs, vws, zs = [], [], []
        for i, nwin in zip(blocks, nwins):
            qstart = i * CHUNK
            kstart = (i - nwin + 1) * CHUNK
            if not isinstance(i, int):
                qstart = pl.multiple_of(qstart, CHUNK)
                kstart = pl.multiple_of(kstart, CHUNK)
            q = q_ref[0, pl.ds(qstart, CHUNK), :].astype(BF16)
            kw = kb_ref[pl.ds(kstart, nwin * CHUNK), :]
            vws.append(vb_ref[pl.ds(kstart, nwin * CHUNK), :])
            z = lax.dot_general(q, kw, (((1,), (1,)), ((), ())),
                                preferred_element_type=F32) * scale
            qs.append(q)
            zs.append([z[:, c * CHUNK:(c + 1) * CHUNK] for c in range(nwin)])
        stats = [[log_survival(zc, c == len(z) - 1) for c, zc in enumerate(z)] for z in zs]
        carries, ws = [], []
        for z, stat in zip(zs, stats):
            nwin = len(z)
            carry = None
            w = [None] * nwin
            for c in reversed(range(nwin)):
                sp, cs = stat[c]
                later = cs[:, :CHUNK] if carry is None else cs[:, :CHUNK] + carry
                w[c] = weights(z[c], sp, later, c == nwin - 1)
                carry = cs[:, CHUNK:] if carry is None else carry + cs[:, CHUNK:]
            carries.append(carry)
            ws.append(w[0] if nwin == 1 else jnp.concatenate(w, axis=1))
        accs = [jnp.dot(w, vw, preferred_element_type=F32) for w, vw in zip(ws, vws)]
        return list(zip(qs, carries, accs))

    def far_chunks(q, c0, carry, acc):
        def cond(s):
            c, carry, _ = s
            return jnp.logical_and(c >= 0, jnp.max(carry) > -EXP_ZERO)

        def body(s):
            c, carry, acc = s
            start = pl.multiple_of(c * CHUNK, CHUNK)
            z = lax.dot_general(q, kb_ref[pl.ds(start, CHUNK), :], (((1,), (1,)), ((), ())),
                                preferred_element_type=F32) * scale
            sp, cs = log_survival(z, False)
            w = weights(z, sp, cs[:, :CHUNK] + carry, False)
            acc = acc + jnp.dot(w, vb_ref[pl.ds(start, CHUNK), :], preferred_element_type=F32)
            return c - 1, carry + cs[:, CHUNK:], acc

        return lax.while_loop(cond, body, (c0, carry, acc))[2]

    def store(i, acc):
        start = i * CHUNK if isinstance(i, int) else pl.multiple_of(i * CHUNK, CHUNK)
        o_ref[0, pl.ds(start, CHUNK), :] = acc.astype(o_ref.dtype)

    first = list(range(WINDOW))
    for i, (_, _, acc) in zip(first, windows(first, [i + 1 for i in first])):
        store(i, acc)

    def group(g, _):
        blocks = [WINDOW + g * GROUP + b for b in range(GROUP)]
        done = windows(blocks, [WINDOW] * GROUP)
        worst = done[0][1]
        for _, carry, _ in done[1:]:
            worst = jnp.maximum(worst, carry)
        for i, (_, _, acc) in zip(blocks, done):
            store(i, acc)

        @pl.when(jnp.max(worst) > -EXP_ZERO)
        def _():
            for i, (q, carry, acc) in zip(blocks, done):
                store(i, far_chunks(q, i - WINDOW, carry, acc))

        return 0

    lax.fori_loop(0, (n_blocks - WINDOW) // GROUP, group, 0)


def _attention(proj3, t_pad):
    b = proj3.shape[0]
    n_blocks = t_pad // CHUNK
    assert n_blocks >= WINDOW and (n_blocks - WINDOW) % GROUP == 0
    blk = (1, t_pad, HEAD)
    return pl.pallas_call(
        functools.partial(_attn_kernel, n_blocks=n_blocks),
        grid=(b, N_HEADS),
        in_specs=[pl.BlockSpec(blk, lambda bi, h: (bi, 0, h)),
                  pl.BlockSpec(blk, lambda bi, h: (bi, 0, N_HEADS + h)),
                  pl.BlockSpec(blk, lambda bi, h: (bi, 0, 2 * N_HEADS + h))],
        out_specs=pl.BlockSpec(blk, lambda bi, h: (bi, 0, h)),
        out_shape=jax.ShapeDtypeStruct((b, t_pad, ATTN_WIDTH), F32),
        scratch_shapes=[pltpu.VMEM((t_pad, HEAD), BF16),
                        pltpu.VMEM((t_pad, HEAD), BF16)],
        compiler_params=_cparams(("arbitrary", "arbitrary")),
        name="stickbreak_attn",
    )(proj3, proj3, proj3)


def _lru_kernel(xr_ref, yg_ref, cw_ref, cb_ref, wa_ref, ba_ref, wi_ref, bi_ref,
                lam_ref, o_ref, x_ref, xc_ref, a_ref, u_ref, *, t_pad, groups):
    rows_p = 8 * groups
    x_ref[0:t_pad, :] = xr_ref[0]
    x_ref[t_pad:rows_p, :] = jnp.zeros((rows_p - t_pad, HEAD), F32)
    sub = lax.broadcasted_iota(jnp.int32, (8, HEAD), 0)
    load = lambda g: x_ref[pl.ds(g, 8, stride=groups), :]
    group = lambda g: slice(8 * g, 8 * g + 8)

    def chunk_down(v, fill):
        return jnp.where(sub >= 1, pltpu.roll(v, 1, 0), fill)

    w0, w1, w2, w3, bias = (jnp.broadcast_to(r, (8, HEAD)) for r in
                            (cw_ref[0:1, :], cw_ref[1:2, :], cw_ref[2:3, :], cw_ref[3:4, :],
                             cb_ref[...]))
    p3, p2, p1 = (chunk_down(load(groups - k), 0.0) for k in (3, 2, 1))
    for g in range(groups):
        cur = load(g)
        xc_ref[group(g), :] = cur * w3 + p1 * w2 + p2 * w1 + p3 * w0 + bias
        p3, p2, p1 = p2, p1, cur

    xc = xc_ref[...]
    xcb = xc.astype(BF16)
    r = jax.nn.sigmoid(jnp.dot(xcb, wa_ref[...].astype(BF16), preferred_element_type=F32)
                       + ba_ref[...])
    gi = jax.nn.sigmoid(jnp.dot(xcb, wi_ref[...].astype(BF16), preferred_element_type=F32)
                        + bi_ref[...])
    log_a = (-LRU_C) * r * _softplus(-lam_ref[...])
    a_ref[...] = jnp.exp(log_a)
    th = jnp.tanh(log_a)
    u_ref[...] = jnp.sqrt(-2.0 * th / (1.0 - th)) * (gi * xc)

    h = jnp.zeros((8, HEAD), F32)
    p = jnp.ones((8, HEAD), F32)
    for g in range(groups):
        a_g = a_ref[group(g), :]
        p = a_g * p
        h = a_g * h + u_ref[group(g), :]
        a_ref[group(g), :] = p
        u_ref[group(g), :] = h

    for d in (1, 2, 4):
        keep = sub >= d
        h = p * jnp.where(keep, pltpu.roll(h, d, 0), 0.0) + h
        p = p * jnp.where(keep, pltpu.roll(p, d, 0), 1.0)
    h_in = chunk_down(h, 0.0)
    for g in range(groups):
        x_ref[pl.ds(g, 8, stride=groups), :] = u_ref[group(g), :] + a_ref[group(g), :] * h_in

    o_ref[0] = x_ref[0:t_pad, :] * jax.nn.gelu(yg_ref[0])


def _lru_branch(proj3, cw, cb, wa, ba, wi, bi, lam, l, t_pad):
    b = proj3.shape[0]
    blk = (1, t_pad, HEAD)
    groups = t_pad // 8 + 1
    groups += 1 - groups % 2
    xr_off = 3 * ATTN_WIDTH // HEAD
    yg_off = xr_off + LRU_WIDTH // HEAD
    vspec = pl.BlockSpec((None, 1, HEAD), lambda bi_, h: (l, 0, h))
    mspec = pl.BlockSpec((None, None, HEAD, HEAD), lambda bi_, h: (l, h, 0, 0))
    return pl.pallas_call(
        functools.partial(_lru_kernel, t_pad=t_pad, groups=groups),
        grid=(b, N_HEADS),
        in_specs=[pl.BlockSpec(blk, lambda bi_, h: (bi_, 0, xr_off + h)),
                  pl.BlockSpec(blk, lambda bi_, h: (bi_, 0, yg_off + h)),
                  pl.BlockSpec((None, 4, HEAD), lambda bi_, h: (l, 0, h)),
                  vspec, mspec, vspec, mspec, vspec, vspec],
        out_specs=pl.BlockSpec(blk, lambda bi_, h: (bi_, 0, h)),
        out_shape=jax.ShapeDtypeStruct((b, t_pad, LRU_WIDTH), F32),
        scratch_shapes=[pltpu.VMEM((8 * groups, HEAD), F32)] * 4,
        compiler_params=_cparams(("arbitrary", "arbitrary")),
        name="rglru",
    )(proj3, proj3, cw, cb, wa, ba, wi, bi, lam)


def _rms_rows(x, g):
    ms = jnp.mean(x * x, axis=-1, keepdims=True)
    return x * lax.rsqrt(ms + RMS_EPS) * g


def _outproj_kernel(oa_ref, ol_ref, ga_ref, gl_ref, w_ref, h_ref, g_ref, b_ref, o_ref, ob_ref):
    na = _rms_rows(oa_ref[...], ga_ref[...]).astype(BF16)
    nl = _rms_rows(ol_ref[...], gl_ref[...]).astype(BF16)
    mix = (jnp.dot(na, w_ref[:ATTN_WIDTH, :], preferred_element_type=F32)
           + jnp.dot(nl, w_ref[ATTN_WIDTH:, :], preferred_element_type=F32))
    y = _layer_norm_rows(DEEPNORM_ALPHA * h_ref[...] + mix, g_ref[...], b_ref[...])
    o_ref[...] = y
    ob_ref[...] = y.astype(BF16)


def _outproj(oa, ol, ga, gl, w, l, h, g, b, bm):
    n, d = h.shape
    row = lambda wdt: pl.BlockSpec((bm, wdt), lambda i: (i, 0))
    vec = lambda wdt: pl.BlockSpec((None, 1, wdt), lambda i: (l, 0, 0))
    return pl.pallas_call(
        _outproj_kernel,
        grid=(n // bm,),
        in_specs=[row(ATTN_WIDTH), row(LRU_WIDTH), vec(ATTN_WIDTH), vec(LRU_WIDTH),
                  pl.BlockSpec((None, d, d), lambda i: (l, 0, 0)), row(d), vec(d), vec(d)],
        out_specs=(row(d), row(d)),
        out_shape=_stream_shapes(n, d),
        compiler_params=_cparams(("arbitrary",)),
        name="outproj_ln1",
    )(oa, ol, ga, gl, w, h, g, b)


def _ffn_up_kernel(x_ref, wg_ref, wv_ref, cwg_ref, cwv_ref, cbg_ref, cbv_ref, o_ref,
                   ug_ref, uv_ref, ug1_ref, uv1_ref, hg_ref, hv_ref, nat_ref, *,
                   nck, tiles_per_seq, pieces):
    s = pl.program_id(0)
    slabs, bm, lanes = ug_ref.shape
    groups = bm // 8

    @pl.when(s == 0)
    def _():
        ug_ref[...] = jnp.zeros_like(ug_ref)
        uv_ref[...] = jnp.zeros_like(uv_ref)
        hg_ref[...] = jnp.zeros_like(hg_ref)
        hv_ref[...] = jnp.zeros_like(hv_ref)

    prev = jnp.maximum(s - 1, 0)
    ip = prev // nck
    jp = prev % nck
    seq_start = (ip % tiles_per_seq) == 0
    sub = lax.broadcasted_iota(jnp.int32, (8, lanes), 0)

    def conv_taps(u_ref, tail_ref, w_ref, b_ref, j):
        cols = slice(j * lanes, (j + 1) * lanes)
        w0, w1, w2, bias = (jnp.broadcast_to(r, (8, lanes)) for r in
                            (w_ref[0:1, cols], w_ref[1:2, cols], w_ref[2:3, cols], b_ref[:, cols]))
        load = lambda g: u_ref[j, pl.ds(g, 8, stride=groups), :]
        last2, last1 = load(groups - 2), load(groups - 1)
        tail2 = jnp.where(seq_start, 0.0, tail_ref[jp, j, 0:8, :])
        tail1 = jnp.where(seq_start, 0.0, tail_ref[jp, j, 8:16, :])
        tail_ref[jp, j, 0:8, :] = last2
        tail_ref[jp, j, 8:16, :] = last1
        before2 = jnp.where(sub >= 1, pltpu.roll(last2, 1, 0), pltpu.roll(tail2, 1, 0))
        before1 = jnp.where(sub >= 1, pltpu.roll(last1, 1, 0), pltpu.roll(tail1, 1, 0))
        conv = lambda cur, b1, b2: cur * w2 + b1 * w1 + b2 * w0 + bias
        return load, conv, before1, before2

    def step(ugp_ref, uvp_ref, ugn_ref, uvn_ref):
        state = []
        for j in range(slabs):
            load_g, conv_g, g1, g2 = conv_taps(ugp_ref, hg_ref, cwg_ref, cbg_ref, j)
            load_v, conv_v, v1, v2 = conv_taps(uvp_ref, hv_ref, cwv_ref, cbv_ref, j)
            state.append([load_g, conv_g, g1, g2, load_v, conv_v, v1, v2])
        for p in range(pieces):
            for j in range(slabs):
                load_g, conv_g, g1, g2, load_v, conv_v, v1, v2 = state[j]
                for g in range(p * groups // pieces, (p + 1) * groups // pieces):
                    cur_g, cur_v = load_g(g), load_v(g)
                    act = jax.nn.silu(conv_g(cur_g, g1, g2)) * conv_v(cur_v, v1, v2)
                    nat_ref[j, pl.ds(g, 8, stride=groups), :] = act
                    g2, g1, v2, v1 = g1, cur_g, v1, cur_v
                state[j] = [load_g, conv_g, g1, g2, load_v, conv_v, v1, v2]
            rows = slice(p * bm // pieces, (p + 1) * bm // pieces)
            x = x_ref[rows, :]
            dg = jnp.dot(x, wg_ref[...], preferred_element_type=F32)
            dv = jnp.dot(x, wv_ref[...], preferred_element_type=F32)
            for j in range(slabs):
                ugn_ref[j, rows, :] = dg[:, j * lanes:(j + 1) * lanes]
                uvn_ref[j, rows, :] = dv[:, j * lanes:(j + 1) * lanes]
        for j in range(slabs):
            o_ref[:, j * lanes:(j + 1) * lanes] = nat_ref[j].astype(o_ref.dtype)

    @pl.when(s % 2 == 0)
    def _():
        step(ug_ref, uv_ref, ug1_ref, uv1_ref)

    @pl.when(s % 2 == 1)
    def _():
        step(ug1_ref, uv1_ref, ug_ref, uv_ref)


def _ffn_up(hb, w_up, conv_w, conv_b, l, bm, ck, tiles_per_seq):
    n, d = hb.shape
    nck = FFN_HIDDEN // ck
    slabs = ck // 128
    pieces = max(p for p in range(1, 9) if bm % (16 * p) == 0 and (bm // p >= 128 or p == 1))
    steps = (n // bm) * nck
    cur = lambda s: jnp.minimum(s, steps - 1)
    prev = lambda s: jnp.maximum(s - 1, 0)
    return pl.pallas_call(
        functools.partial(_ffn_up_kernel, nck=nck, tiles_per_seq=tiles_per_seq, pieces=pieces),
        grid=(steps + 1,),
        in_specs=[pl.BlockSpec((bm, d), lambda s: (cur(s) // nck, 0)),
                  pl.BlockSpec((None, d, ck), lambda s: (l, 0, cur(s) % nck)),
                  pl.BlockSpec((None, d, ck), lambda s: (l, 0, nck + cur(s) % nck)),
                  pl.BlockSpec((None, 3, ck), lambda s: (l, 0, prev(s) % nck)),
                  pl.BlockSpec((None, 3, ck), lambda s: (l, 0, nck + prev(s) % nck)),
                  pl.BlockSpec((None, 1, ck), lambda s: (l, 0, prev(s) % nck)),
                  pl.BlockSpec((None, 1, ck), lambda s: (l, 0, nck + prev(s) % nck))],
        out_specs=pl.BlockSpec((bm, ck), lambda s: (prev(s) // nck, prev(s) % nck)),
        out_shape=jax.ShapeDtypeStruct((n, FFN_HIDDEN), BF16),
        scratch_shapes=[pltpu.VMEM((slabs, bm, 128), F32),
                        pltpu.VMEM((slabs, bm, 128), F32),
                        pltpu.VMEM((slabs, bm, 128), F32),
                        pltpu.VMEM((slabs, bm, 128), F32),
                        pltpu.VMEM((nck, slabs, 16, 128), F32),
                        pltpu.VMEM((nck, slabs, 16, 128), F32),
                        pltpu.VMEM((slabs, bm, 128), F32)],
        compiler_params=_cparams(("arbitrary",)),
        name="ffn_up_conv_glu",
    )(hb, w_up, w_up, conv_w, conv_w, conv_b, conv_b)


def _ffn_down_kernel(a_ref, w_ref, h_ref, g_ref, b_ref, o_ref, ob_ref):
    o_ref[...] = jnp.dot(a_ref[...], w_ref[...], preferred_element_type=F32)
    for r in range(0, o_ref.shape[0], LN_SLAB):
        rows = pl.ds(r, LN_SLAB)
        y = _layer_norm_rows(DEEPNORM_ALPHA * h_ref[rows, :] + o_ref[rows, :],
                             g_ref[...], b_ref[...])
        o_ref[rows, :] = y
        ob_ref[rows, :] = y.astype(BF16)


def _ffn_down(act, w, l, h, g, b, bm):
    n, d = h.shape
    assert bm % LN_SLAB == 0 and n % bm == 0
    row = pl.BlockSpec((bm, d), lambda i: (i, 0))
    vec = pl.BlockSpec((None, 1, d), lambda i: (l, 0, 0))
    return pl.pallas_call(
        _ffn_down_kernel,
        grid=(n // bm,),
        in_specs=[pl.BlockSpec((bm, FFN_HIDDEN), lambda i: (i, 0)),
                  pl.BlockSpec((None, FFN_HIDDEN, d), lambda i: (l, 0, 0),
                               pipeline_mode=pl.Buffered(1)),
                  row, vec, vec],
        out_specs=(row, row),
        out_shape=_stream_shapes(n, d),
        compiler_params=_cparams(("arbitrary",)),
        name="ffn_down_ln2",
    )(act, w, h, g, b)


def kernel(x, meta_tokens, ln0_g, ln0_b, w_in, lru_conv_w, lru_conv_b, lru_w_a, lru_b_a, lru_w_i, lru_b_i, lru_lambda, g_attn, g_lru, w_out, ln1_g, ln1_b, ffn_w_up, ffn_conv_w, ffn_conv_b, ffn_w_down, ln2_g, ln2_b):
    b, seq, d = x.shape
    t = N_META + seq
    t_pad = -(-t // CHUNK) * CHUNK
    tiles_per_seq = 4
    bm = t_pad // tiles_per_seq
    n = b * t_pad

    meta = jnp.broadcast_to(meta_tokens[None].astype(x.dtype), (b, N_META, d))
    tail = jnp.zeros((b, t_pad - t, d), x.dtype)
    h = jnp.concatenate([meta, x, tail], axis=1).reshape(n, d)
    h, hb = _ln_rows(h, ln0_g, ln0_b, bm // 2)

    vec = lambda a: a.reshape(DEPTH, 1, -1)
    w_in_b, w_out_b = w_in.astype(BF16), w_out.astype(BF16)
    w_up_b, w_down_b = ffn_w_up.astype(BF16), ffn_w_down.astype(BF16)
    lru_vecs = [vec(a) for a in (lru_conv_b, lru_b_a, lru_b_i, lru_lambda)]
    g_attn, g_lru, ln1_g, ln1_b, ln2_g, ln2_b, ffn_conv_b = (
        vec(a) for a in (g_attn, g_lru, ln1_g, ln1_b, ln2_g, ln2_b, ffn_conv_b))

    for l in range(DEPTH):
        proj = _inproj(hb, w_in_b, l, bm, 1024)
        proj3 = proj.reshape(b, t_pad, IN_COLS)
        o_attn = _attention(proj3, t_pad)
        o_lru = _lru_branch(proj3, lru_conv_w, lru_vecs[0], lru_w_a, lru_vecs[1],
                            lru_w_i, lru_vecs[2], lru_vecs[3], l, t_pad)
        h, hb = _outproj(o_attn.reshape(n, ATTN_WIDTH), o_lru.reshape(n, LRU_WIDTH),
                         g_attn, g_lru, w_out_b, l, h, ln1_g, ln1_b, bm // 2)
        act = _ffn_up(hb, w_up_b, ffn_conv_w, ffn_conv_b, l, bm, 512, tiles_per_seq)
        h, hb = _ffn_down(act, w_down_b, l, h, ln2_g, ln2_b, bm // 3)

    return h.reshape(b, t_pad, d)[:, N_META:t]
```

```python
import functools

import jax
import jax.numpy as jnp
from jax import lax
from jax.experimental import pallas as pl
from jax.experimental.pallas import tpu as pltpu

F32 = jnp.float32
BF16 = jnp.bfloat16

D_MODEL = 2048
DEPTH = 4
N_META = 16
HEAD = 128
ATTN_WIDTH = 1024
LRU_WIDTH = 1024
N_HEADS = 8
IN_COLS = 3 * ATTN_WIDTH + 2 * LRU_WIDTH
FFN_HIDDEN = 5632
LRU_C = 8.0
DEEPNORM_ALPHA = (2 * DEPTH) ** 0.25
LN_EPS = 1e-5
RMS_EPS = 1e-6

CHUNK = 128
EXP_ZERO = 104.0
WINDOW = 3
GROUP = 10
LN_SLAB = 32

VMEM_LIMIT = 56 * 1024 * 1024


def _cparams(sem):
    return pltpu.CompilerParams(dimension_semantics=sem, vmem_limit_bytes=VMEM_LIMIT)


def _layer_norm_rows(y, g, b):
    mu = jnp.mean(y, axis=-1, keepdims=True)
    yc = y - mu
    var = jnp.mean(yc * yc, axis=-1, keepdims=True)
    return yc * lax.rsqrt(var + LN_EPS) * g + b


def _softplus(x):
    return jnp.maximum(x, 0.0) + jnp.log(1.0 + jnp.exp(-jnp.abs(x)))


def _stream_shapes(n, d):
    return (jax.ShapeDtypeStruct((n, d), F32), jax.ShapeDtypeStruct((n, d), BF16))


def _ln_kernel(x_ref, g_ref, b_ref, o_ref, ob_ref):
    y = _layer_norm_rows(x_ref[...], g_ref[...], b_ref[...])
    o_ref[...] = y
    ob_ref[...] = y.astype(BF16)


def _ln_rows(x, g, b, bm):
    n, d = x.shape
    row = pl.BlockSpec((bm, d), lambda i: (i, 0))
    vec = pl.BlockSpec((1, d), lambda i: (0, 0))
    return pl.pallas_call(
        _ln_kernel,
        grid=(n // bm,),
        in_specs=[row, vec, vec],
        out_specs=(row, row),
        out_shape=_stream_shapes(n, d),
        compiler_params=_cparams(("arbitrary",)),
        name="ln0",
    )(x, g.reshape(1, d), b.reshape(1, d))


def _inproj_kernel(h_ref, w_ref, o_ref):
    o_ref[...] = jnp.dot(h_ref[...], w_ref[...], preferred_element_type=F32)


def _inproj(hb, w, l, bm, bn):
    n, d = hb.shape
    cols = w.shape[-1]
    return pl.pallas_call(
        _inproj_kernel,
        grid=(n // bm, cols // bn),
        in_specs=[pl.BlockSpec((bm, d), lambda i, j: (i, 0)),
                  pl.BlockSpec((None, d, bn), lambda i, j: (l, 0, j))],
        out_specs=pl.BlockSpec((bm, bn), lambda i, j: (i, j)),
        out_shape=jax.ShapeDtypeStruct((n, cols), F32),
        compiler_params=_cparams(("arbitrary", "arbitrary")),
        name="inproj",
    )(hb, w)


def _attn_kernel(q_ref, k_ref, v_ref, o_ref, kb_ref, vb_ref, *, n_blocks):
    kb_ref[...] = k_ref[0].astype(BF16)
    vb_ref[...] = v_ref[0].astype(BF16)
    scale = HEAD ** -0.5

    row = lax.broadcasted_iota(jnp.int32, (CHUNK, CHUNK), 0)
    col = lax.broadcasted_iota(jnp.int32, (CHUNK, CHUNK), 1)
    strictly_earlier = col < row
    jj = lax.broadcasted_iota(jnp.int32, (2 * CHUNK, 2 * CHUNK), 0)
    ss = lax.broadcasted_iota(jnp.int32, (2 * CHUNK, 2 * CHUNK), 1)
    minus_suffix = jnp.where((ss >= CHUNK) | ((jj & (CHUNK - 1)) > ss), -1.0, 0.0).astype(BF16)

    def log_survival(z, diagonal):
        sp = _softplus(z)
        m = jnp.where(strictly_earlier, sp, 0.0) if diagonal else sp
        m_hi = m.astype(BF16)
        m_lo = (m - m_hi.astype(F32)).astype(BF16)
        cs = jnp.dot(jnp.concatenate([m_hi, m_lo], axis=1), minus_suffix,
                     preferred_element_type=F32)
        return sp, cs

    def weights(z, sp, later, diagonal):
        w = jnp.exp(z - sp + later)
        if diagonal:
            w = jnp.where(strictly_earlier, w, 0.0)
        return w.astype(BF16)

    def windows(blocks, nwins):
        qs, vws, zs = [], [], []
        for i, nwin in zip(blocks, nwins):
            qstart = i * CHUNK
            kstart = (i - nwin + 1) * CHUNK
            if not isinstance(i, int):
                qstart = pl.multiple_of(qstart, CHUNK)
                kstart = pl.multiple_of(kstart, CHUNK)
            q = q_ref[0, pl.ds(qstart, CHUNK), :].astype(BF16)
            kw = kb_ref[pl.ds(kstart, nwin * CHUNK), :]
            vws.append(vb_ref[pl.ds(kstart, nwin * CHUNK), :])
            z = lax.dot_general(q, kw, (((1,), (1,)), ((), ())),
                                preferred_element_type=F32) * scale
            qs.append(q)
            zs.append([z[:, c * CHUNK:(c + 1) * CHUNK] for c in range(nwin)])
        stats = [[log_survival(zc, c == len(z) - 1) for c, zc in enumerate(z)] for z in zs]
        carries, ws = [], []
        for z, stat in zip(zs, stats):
            nwin = len(z)
            carry = None
            w = [None] * nwin
            for c in reversed(range(nwin)):
                sp, cs = stat[c]
                later = cs[:, :CHUNK] if carry is None else cs[:, :CHUNK] + carry
                w[c] = weights(z[c], sp, later, c == nwin - 1)
                carry = cs[:, CHUNK:] if carry is None else carry + cs[:, CHUNK:]
            carries.append(carry)
            ws.append(w[0] if nwin == 1 else jnp.concatenate(w, axis=1))
        accs = [jnp.dot(w, vw, preferred_element_type=F32) for w, vw in zip(ws, vws)]
        return list(zip(qs, carries, accs))

    def far_chunks(q, c0, carry, acc):
        def cond(s):
            c, carry, _ = s
            return jnp.logical_and(c >= 0, jnp.max(carry) > -EXP_ZERO)

        def body(s):
            c, carry, acc = s
            start = pl.multiple_of(c * CHUNK, CHUNK)
            z = lax.dot_general(q, kb_ref[pl.ds(start, CHUNK), :], (((1,), (1,)), ((), ())),
                                preferred_element_type=F32) * scale
            sp, cs = log_survival(z, False)
            w = weights(z, sp, cs[:, :CHUNK] + carry, False)
            acc = acc + jnp.dot(w, vb_ref[pl.ds(start, CHUNK), :], preferred_element_type=F32)
            return c - 1, carry + cs[:, CHUNK:], acc

        return lax.while_loop(cond, body, (c0, carry, acc))[2]

    def store(i, acc):
        start = i * CHUNK if isinstance(i, int) else pl.multiple_of(i * CHUNK, CHUNK)
        o_ref[0, pl.ds(start, CHUNK), :] = acc.astype(o_ref.dtype)

    first = list(range(WINDOW))
    for i, (_, _, acc) in zip(first, windows(first, [i + 1 for i in first])):
        store(i, acc)

    def group(g, _):
        blocks = [WINDOW + g * GROUP + b for b in range(GROUP)]
        done = windows(blocks, [WINDOW] * GROUP)
        worst = done[0][1]
        for _, carry, _ in done[1:]:
            worst = jnp.maximum(worst, carry)
        for i, (_, _, acc) in zip(blocks, done):
            store(i, acc)

        @pl.when(jnp.max(worst) > -EXP_ZERO)
        def _():
            for i, (q, carry, acc) in zip(blocks, done):
                store(i, far_chunks(q, i - WINDOW, carry, acc))

        return 0

    lax.fori_loop(0, (n_blocks - WINDOW) // GROUP, group, 0)


def _attention(proj3, t_pad):
    b = proj3.shape[0]
    n_blocks = t_pad // CHUNK
    assert n_blocks >= WINDOW and (n_blocks - WINDOW) % GROUP == 0
    blk = (1, t_pad, HEAD)
    return pl.pallas_call(
        functools.partial(_attn_kernel, n_blocks=n_blocks),
        grid=(b, N_HEADS),
        in_specs=[pl.BlockSpec(blk, lambda bi, h: (bi, 0, h)),
                  pl.BlockSpec(blk, lambda bi, h: (bi, 0, N_HEADS + h)),
                  pl.BlockSpec(blk, lambda bi, h: (bi, 0, 2 * N_HEADS + h))],
        out_specs=pl.BlockSpec(blk, lambda bi, h: (bi, 0, h)),
        out_shape=jax.ShapeDtypeStruct((b, t_pad, ATTN_WIDTH), F32),
        scratch_shapes=[pltpu.VMEM((t_pad, HEAD), BF16),
                        pltpu.VMEM((t_pad, HEAD), BF16)],
        compiler_params=_cparams(("arbitrary", "arbitrary")),
        name="stickbreak_attn",
    )(proj3, proj3, proj3)


def _lru_kernel(xr_ref, yg_ref, cw_ref, cb_ref, wa_ref, ba_ref, wi_ref, bi_ref,
                lam_ref, o_ref, x_ref, xc_ref, a_ref, u_ref, *, t_pad, groups):
    rows_p = 8 * groups
    x_ref[0:t_pad, :] = xr_ref[0]
    x_ref[t_pad:rows_p, :] = jnp.zeros((rows_p - t_pad, HEAD), F32)
    sub = lax.broadcasted_iota(jnp.int32, (8, HEAD), 0)
    load = lambda g: x_ref[pl.ds(g, 8, stride=groups), :]
    group = lambda g: slice(8 * g, 8 * g + 8)

    def chunk_down(v, fill):
        return jnp.where(sub >= 1, pltpu.roll(v, 1, 0), fill)

    w0, w1, w2, w3, bias = (jnp.broadcast_to(r, (8, HEAD)) for r in
                            (cw_ref[0:1, :], cw_ref[1:2, :], cw_ref[2:3, :], cw_ref[3:4, :],
                             cb_ref[...]))
    p3, p2, p1 = (chunk_down(load(groups - k), 0.0) for k in (3, 2, 1))
    for g in range(groups):
        cur = load(g)
        xc_ref[group(g), :] = cur * w3 + p1 * w2 + p2 * w1 + p3 * w0 + bias
        p3, p2, p1 = p2, p1, cur

    xc = xc_ref[...]
    xcb = xc.astype(BF16)
    r = jax.nn.sigmoid(jnp.dot(xcb, wa_ref[...].astype(BF16), preferred_element_type=F32)
                       + ba_ref[...])
    gi = jax.nn.sigmoid(jnp.dot(xcb, wi_ref[...].astype(BF16), preferred_element_type=F32)
                        + bi_ref[...])
    log_a = (-LRU_C) * r * _softplus(-lam_ref[...])
    a_ref[...] = jnp.exp(log_a)
    th = jnp.tanh(log_a)
    u_ref[...] = jnp.sqrt(-2.0 * th / (1.0 - th)) * (gi * xc)

    h = jnp.zeros((8, HEAD), F32)
    p = jnp.ones((8, HEAD), F32)
    for g in range(groups):
        a_g = a_ref[group(g), :]
        p = a_g * p
        h = a_g * h + u_ref[group(g), :]
        a_ref[group(g), :] = p
        u_ref[group(g), :] = h

    for d in (1, 2, 4):
        keep = sub >= d
        h = p * jnp.where(keep, pltpu.roll(h, d, 0), 0.0) + h
        p = p * jnp.where(keep, pltpu.roll(p, d, 0), 1.0)
    h_in = chunk_down(h, 0.0)
    for g in range(groups):
        x_ref[pl.ds(g, 8, stride=groups), :] = u_ref[group(g), :] + a_ref[group(g), :] * h_in

    o_ref[0] = x_ref[0:t_pad, :] * jax.nn.gelu(yg_ref[0])


def _lru_branch(proj3, cw, cb, wa, ba, wi, bi, lam, l, t_pad):
    b = proj3.shape[0]
    blk = (1, t_pad, HEAD)
    groups = t_pad // 8 + 1
    groups += 1 - groups % 2
    xr_off = 3 * ATTN_WIDTH // HEAD
    yg_off = xr_off + LRU_WIDTH // HEAD
    vspec = pl.BlockSpec((None, 1, HEAD), lambda bi_, h: (l, 0, h))
    mspec = pl.BlockSpec((None, None, HEAD, HEAD), lambda bi_, h: (l, h, 0, 0))
    return pl.pallas_call(
        functools.partial(_lru_kernel, t_pad=t_pad, groups=groups),
        grid=(b, N_HEADS),
        in_specs=[pl.BlockSpec(blk, lambda bi_, h: (bi_, 0, xr_off + h)),
                  pl.BlockSpec(blk, lambda bi_, h: (bi_, 0, yg_off + h)),
                  pl.BlockSpec((None, 4, HEAD), lambda bi_, h: (l, 0, h)),
                  vspec, mspec, vspec, mspec, vspec, vspec],
        out_specs=pl.BlockSpec(blk, lambda bi_, h: (bi_, 0, h)),
        out_shape=jax.ShapeDtypeStruct((b, t_pad, LRU_WIDTH), F32),
        scratch_shapes=[pltpu.VMEM((8 * groups, HEAD), F32)] * 4,
        compiler_params=_cparams(("arbitrary", "arbitrary")),
        name="rglru",
    )(proj3, proj3, cw, cb, wa, ba, wi, bi, lam)


def _rms_rows(x, g):
    ms = jnp.mean(x * x, axis=-1, keepdims=True)
    return x * lax.rsqrt(ms + RMS_EPS) * g


def _outproj_kernel(oa_ref, ol_ref, ga_ref, gl_ref, w_ref, h_ref, g_ref, b_ref, o_ref, ob_ref):
    na = _rms_rows(oa_ref[...], ga_ref[...]).astype(BF16)
    nl = _rms_rows(ol_ref[...], gl_ref[...]).astype(BF16)
    mix = (jnp.dot(na, w_ref[:ATTN_WIDTH, :], preferred_element_type=F32)
           + jnp.dot(nl, w_ref[ATTN_WIDTH:, :], preferred_element_type=F32))
    y = _layer_norm_rows(DEEPNORM_ALPHA * h_ref[...] + mix, g_ref[...], b_ref[...])
    o_ref[...] = y
    ob_ref[...] = y.astype(BF16)


def _outproj(oa, ol, ga, gl, w, l, h, g, b, bm):
    n, d = h.shape
    row = lambda wdt: pl.BlockSpec((bm, wdt), lambda i: (i, 0))
    vec = lambda wdt: pl.BlockSpec((None, 1, wdt), lambda i: (l, 0, 0))
    return pl.pallas_call(
        _outproj_kernel,
        grid=(n // bm,),
        in_specs=[row(ATTN_WIDTH), row(LRU_WIDTH), vec(ATTN_WIDTH), vec(LRU_WIDTH),
                  pl.BlockSpec((None, d, d), lambda i: (l, 0, 0)), row(d), vec(d), vec(d)],
        out_specs=(row(d), row(d)),
        out_shape=_stream_shapes(n, d),
        compiler_params=_cparams(("arbitrary",)),
        name="outproj_ln1",
    )(oa, ol, ga, gl, w, h, g, b)


def _ffn_up_kernel(x_ref, wg_ref, wv_ref, cwg_ref, cwv_ref, cbg_ref, cbv_ref, o_ref,
                   ug_ref, uv_ref, ug1_ref, uv1_ref, hg_ref, hv_ref, nat_ref, *,
                   nck, tiles_per_seq, pieces):
    s = pl.program_id(0)
    slabs, bm, lanes = ug_ref.shape
    groups = bm // 8

    @pl.when(s == 0)
    def _():
        ug_ref[...] = jnp.zeros_like(ug_ref)
        uv_ref[...] = jnp.zeros_like(uv_ref)
        hg_ref[...] = jnp.zeros_like(hg_ref)
        hv_ref[...] = jnp.zeros_like(hv_ref)

    prev = jnp.maximum(s - 1, 0)
    ip = prev // nck
    jp = prev % nck
    seq_start = (ip % tiles_per_seq) == 0
    sub = lax.broadcasted_iota(jnp.int32, (8, lanes), 0)

    def conv_taps(u_ref, tail_ref, w_ref, b_ref, j):
        cols = slice(j * lanes, (j + 1) * lanes)
        w0, w1, w2, bias = (jnp.broadcast_to(r, (8, lanes)) for r in
                            (w_ref[0:1, cols], w_ref[1:2, cols], w_ref[2:3, cols], b_ref[:, cols]))
        load = lambda g: u_ref[j, pl.ds(g, 8, stride=groups), :]
        last2, last1 = load(groups - 2), load(groups - 1)
        tail2 = jnp.where(seq_start, 0.0, tail_ref[jp, j, 0:8, :])
        tail1 = jnp.where(seq_start, 0.0, tail_ref[jp, j, 8:16, :])
        tail_ref[jp, j, 0:8, :] = last2
        tail_ref[jp, j, 8:16, :] = last1
        before2 = jnp.where(sub >= 1, pltpu.roll(last2, 1, 0), pltpu.roll(tail2, 1, 0))
        before1 = jnp.where(sub >= 1, pltpu.roll(last1, 1, 0), pltpu.roll(tail1, 1, 0))
        conv = lambda cur, b1, b2: cur * w2 + b1 * w1 + b2 * w0 + bias
        return load, conv, before1, before2

    def step(ugp_ref, uvp_ref, ugn_ref, uvn_ref):
        state = []
        for j in range(slabs):
            load_g, conv_g, g1, g2 = conv_taps(ugp_ref, hg_ref, cwg_ref, cbg_ref, j)
            load_v, conv_v, v1, v2 = conv_taps(uvp_ref, hv_ref, cwv_ref, cbv_ref, j)
            state.append([load_g, conv_g, g1, g2, load_v, conv_v, v1, v2])
        for p in range(pieces):
            for j in range(slabs):
                load_g, conv_g, g1, g2, load_v, conv_v, v1, v2 = state[j]
                for g in range(p * groups // pieces, (p + 1) * groups // pieces):
                    cur_g, cur_v = load_g(g), load_v(g)
                    act = jax.nn.silu(conv_g(cur_g, g1, g2)) * conv_v(cur_v, v1, v2)
                    nat_ref[j, pl.ds(g, 8, stride=groups), :] = act
                    g2, g1, v2, v1 = g1, cur_g, v1, cur_v
                state[j] = [load_g, conv_g, g1, g2, load_v, conv_v, v1, v2]
            rows = slice(p * bm // pieces, (p + 1) * bm // pieces)
            x = x_ref[rows, :]
            dg = jnp.dot(x, wg_ref[...], preferred_element_type=F32)
            dv = jnp.dot(x, wv_ref[...], preferred_element_type=F32)
            for j in range(slabs):
                ugn_ref[j, rows, :] = dg[:, j * lanes:(j + 1) * lanes]
                uvn_ref[j, rows, :] = dv[:, j * lanes:(j + 1) * lanes]
        for j in range(slabs):
            o_ref[:, j * lanes:(j + 1) * lanes] = nat_ref[j].astype(o_ref.dtype)

    @pl.when(s % 2 == 0)
    def _():
        step(ug_ref, uv_ref, ug1_ref, uv1_ref)

    @pl.when(s % 2 == 1)
    def _():
        step(ug1_ref, uv1_ref, ug_ref, uv_ref)


def _ffn_up(hb, w_up, conv_w, conv_b, l, bm, ck, tiles_per_seq):
    n, d = hb.shape
    nck = FFN_HIDDEN // ck
    slabs = ck // 128
    pieces = max(p for p in range(1, 4) if bm % (16 * p) == 0 and (bm // p >= 128 or p == 1))
    steps = (n // bm) * nck
    cur = lambda s: jnp.minimum(s, steps - 1)
    prev = lambda s: jnp.maximum(s - 1, 0)
    return pl.pallas_call(
        functools.partial(_ffn_up_kernel, nck=nck, tiles_per_seq=tiles_per_seq, pieces=pieces),
        grid=(steps + 1,),
        in_specs=[pl.BlockSpec((bm, d), lambda s: (cur(s) // nck, 0)),
                  pl.BlockSpec((None, d, ck), lambda s: (l, 0, cur(s) % nck)),
                  pl.BlockSpec((None, d, ck), lambda s: (l, 0, nck + cur(s) % nck)),
                  pl.BlockSpec((None, 3, ck), lambda s: (l, 0, prev(s) % nck)),
                  pl.BlockSpec((None, 3, ck), lambda s: (l, 0, nck + prev(s) % nck)),
                  pl.BlockSpec((None, 1, ck), lambda s: (l, 0, prev(s) % nck)),
                  pl.BlockSpec((None, 1, ck), lambda s: (l, 0, nck + prev(s) % nck))],
        out_specs=pl.BlockSpec((bm, ck), lambda s: (prev(s) // nck, prev(s) % nck)),
        out_shape=jax.ShapeDtypeStruct((n, FFN_HIDDEN), BF16),
        scratch_shapes=[pltpu.VMEM((slabs, bm, 128), F32),
                        pltpu.VMEM((slabs, bm, 128), F32),
                        pltpu.VMEM((slabs, bm, 128), F32),
                        pltpu.VMEM((slabs, bm, 128), F32),
                        pltpu.VMEM((nck, slabs, 16, 128), F32),
                        pltpu.VMEM((nck, slabs, 16, 128), F32),
                        pltpu.VMEM((slabs, bm, 128), F32)],
        compiler_params=_cparams(("arbitrary",)),
        name="ffn_up_conv_glu",
    )(hb, w_up, w_up, conv_w, conv_w, conv_b, conv_b)


def _ffn_down_kernel(a_ref, w_ref, h_ref, g_ref, b_ref, o_ref, ob_ref):
    o_ref[...] = jnp.dot(a_ref[...], w_ref[...], preferred_element_type=F32)
    for r in range(0, o_ref.shape[0], LN_SLAB):
        rows = pl.ds(r, LN_SLAB)
        y = _layer_norm_rows(DEEPNORM_ALPHA * h_ref[rows, :] + o_ref[rows, :],
                             g_ref[...], b_ref[...])
        o_ref[rows, :] = y
        ob_ref[rows, :] = y.astype(BF16)


def _ffn_down(act, w, l, h, g, b, bm):
    n, d = h.shape
    assert bm % LN_SLAB == 0 and n % bm == 0
    row = pl.BlockSpec((bm, d), lambda i: (i, 0))
    vec = pl.BlockSpec((None, 1, d), lambda i: (l, 0, 0))
    return pl.pallas_call(
        _ffn_down_kernel,
        grid=(n // bm,),
        in_specs=[pl.BlockSpec((bm, FFN_HIDDEN), lambda i: (i, 0)),
                  pl.BlockSpec((None, FFN_HIDDEN, d), lambda i: (l, 0, 0),
                               pipeline_mode=pl.Buffered(1)),
                  row, vec, vec],
        out_specs=(row, row),
        out_shape=_stream_shapes(n, d),
        compiler_params=_cparams(("arbitrary",)),
        name="ffn_down_ln2",
    )(act, w, h, g, b)


def kernel(x, meta_tokens, ln0_g, ln0_b, w_in, lru_conv_w, lru_conv_b, lru_w_a, lru_b_a, lru_w_i, lru_b_i, lru_lambda, g_attn, g_lru, w_out, ln1_g, ln1_b, ffn_w_up, ffn_conv_w, ffn_conv_b, ffn_w_down, ln2_g, ln2_b):
    b, seq, d = x.shape
    t = N_META + seq
    t_pad = -(-t // CHUNK) * CHUNK
    tiles_per_seq = 4
    bm = t_pad // tiles_per_seq
    n = b * t_pad

    meta = jnp.broadcast_to(meta_tokens[None].astype(x.dtype), (b, N_META, d))
    tail = jnp.zeros((b, t_pad - t, d), x.dtype)
    h = jnp.concatenate([meta, x, tail], axis=1).reshape(n, d)
    h, hb = _ln_rows(h, ln0_g, ln0_b, bm // 2)

    vec = lambda a: a.reshape(DEPTH, 1, -1)
    w_in_b, w_out_b = w_in.astype(BF16), w_out.astype(BF16)
    w_up_b, w_down_b = ffn_w_up.astype(BF16), ffn_w_down.astype(BF16)
    lru_vecs = [vec(a) for a in (lru_conv_b, lru_b_a, lru_b_i, lru_lambda)]
    g_attn, g_lru, ln1_g, ln1_b, ln2_g, ln2_b, ffn_conv_b = (
        vec(a) for a in (g_attn, g_lru, ln1_g, ln1_b, ln2_g, ln2_b, ffn_conv_b))

    for l in range(DEPTH):
        proj = _inproj(hb, w_in_b, l, bm, 1024)
        proj3 = proj.reshape(b, t_pad, IN_COLS)
        o_attn = _attention(proj3, t_pad)
        o_lru = _lru_branch(proj3, lru_conv_w, lru_vecs[0], lru_w_a, lru_vecs[1],
                            lru_w_i, lru_vecs[2], lru_vecs[3], l, t_pad)
        h, hb = _outproj(o_attn.reshape(n, ATTN_WIDTH), o_lru.reshape(n, LRU_WIDTH),
                         g_attn, g_lru, w_out_b, l, h, ln1_g, ln1_b, bm // 2)
        act = _ffn_up(hb, w_up_b, ffn_conv_w, ffn_conv_b, l, bm, 512, tiles_per_seq)
        h, hb = _ffn_down(act, w_down_b, l, h, ln2_g, ln2_b, bm // 3)

    return h.reshape(b, t_pad, d)[:, N_META:t]
```

```python
import functools

import jax
import jax.numpy as jnp
from jax import lax
from jax.experimental import pallas as pl
from jax.experimental.pallas import tpu as pltpu

F32 = jnp.float32
BF16 = jnp.bfloat16

D_MODEL = 2048
DEPTH = 4
N_META = 16
HEAD = 128
ATTN_WIDTH = 1024
LRU_WIDTH = 1024
N_HEADS = 8
IN_COLS = 3 * ATTN_WIDTH + 2 * LRU_WIDTH
FFN_HIDDEN = 5632
LRU_C = 8.0
DEEPNORM_ALPHA = (2 * DEPTH) ** 0.25
LN_EPS = 1e-5
RMS_EPS = 1e-6

CHUNK = 128
EXP_ZERO = 104.0
LOG2E = 1.4426950408889634
WINDOW = 3
GROUP = 10
LN_SLAB = 32

VMEM_LIMIT = 56 * 1024 * 1024


def _cparams(sem):
    return pltpu.CompilerParams(dimension_semantics=sem, vmem_limit_bytes=VMEM_LIMIT)


def _layer_norm_rows(y, g, b):
    mu = jnp.mean(y, axis=-1, keepdims=True)
    yc = y - mu
    var = jnp.mean(yc * yc, axis=-1, keepdims=True)
    return yc * lax.rsqrt(var + LN_EPS) * g + b


def _softplus(x):
    return jnp.maximum(x, 0.0) + jnp.log(1.0 + jnp.exp(-jnp.abs(x)))


def _stream_shapes(n, d):
    return (jax.ShapeDtypeStruct((n, d), F32), jax.ShapeDtypeStruct((n, d), BF16))


def _ln_kernel(x_ref, g_ref, b_ref, o_ref, ob_ref):
    y = _layer_norm_rows(x_ref[...], g_ref[...], b_ref[...])
    o_ref[...] = y
    ob_ref[...] = y.astype(BF16)


def _ln_rows(x, g, b, bm):
    n, d = x.shape
    row = pl.BlockSpec((bm, d), lambda i: (i, 0))
    vec = pl.BlockSpec((1, d), lambda i: (0, 0))
    return pl.pallas_call(
        _ln_kernel,
        grid=(n // bm,),
        in_specs=[row, vec, vec],
        out_specs=(row, row),
        out_shape=_stream_shapes(n, d),
        compiler_params=_cparams(("arbitrary",)),
        name="ln0",
    )(x, g.reshape(1, d), b.reshape(1, d))


def _inproj_kernel(h_ref, w_ref, qkv_ref, lru_ref, *, qkv_blocks):
    j = pl.program_id(1)
    product = lambda: jnp.dot(h_ref[...], w_ref[...], preferred_element_type=F32)

    @pl.when(j < qkv_blocks)
    def _():
        qkv_ref[...] = product().astype(BF16)

    @pl.when(j >= qkv_blocks)
    def _():
        lru_ref[...] = product()


def _inproj(hb, w, l, bm, bn):
    n, d = hb.shape
    qkv_cols, lru_cols = 3 * ATTN_WIDTH, 2 * LRU_WIDTH
    assert w.shape[-1] == qkv_cols + lru_cols and qkv_cols % bn == 0 and lru_cols % bn == 0
    qkv_blocks = qkv_cols // bn
    return pl.pallas_call(
        functools.partial(_inproj_kernel, qkv_blocks=qkv_blocks),
        grid=(n // bm, (qkv_cols + lru_cols) // bn),
        in_specs=[pl.BlockSpec((bm, d), lambda i, j: (i, 0)),
                  pl.BlockSpec((None, d, bn), lambda i, j: (l, 0, j))],
        out_specs=(pl.BlockSpec((bm, bn), lambda i, j: (i, jnp.minimum(j, qkv_blocks - 1))),
                   pl.BlockSpec((bm, bn), lambda i, j: (i, jnp.maximum(j - qkv_blocks, 0)))),
        out_shape=(jax.ShapeDtypeStruct((n, qkv_cols), BF16),
                   jax.ShapeDtypeStruct((n, lru_cols), F32)),
        compiler_params=_cparams(("arbitrary", "arbitrary")),
        name="inproj",
    )(hb, w)


def _attn_kernel(q_ref, k_ref, v_ref, o_ref, *, n_blocks):
    scale = HEAD ** -0.5 * LOG2E

    row = lax.broadcasted_iota(jnp.int32, (CHUNK, CHUNK), 0)
    col = lax.broadcasted_iota(jnp.int32, (CHUNK, CHUNK), 1)
    strictly_earlier = col < row
    jj = lax.broadcasted_iota(jnp.int32, (2 * CHUNK, 2 * CHUNK), 0)
    ss = lax.broadcasted_iota(jnp.int32, (2 * CHUNK, 2 * CHUNK), 1)
    minus_suffix = jnp.where((ss >= CHUNK) | ((jj & (CHUNK - 1)) > ss), -1.0, 0.0).astype(BF16)

    def log_survival(z, diagonal):
        sp = jnp.maximum(z, 0.0) + jnp.log2(1.0 + jnp.exp2(-jnp.abs(z)))
        m = jnp.where(strictly_earlier, sp, 0.0) if diagonal else sp
        m_hi = m.astype(BF16)
        m_lo = (m - m_hi.astype(F32)).astype(BF16)
        cs = jnp.dot(jnp.concatenate([m_hi, m_lo], axis=1), minus_suffix,
                     preferred_element_type=F32)
        return sp, cs

    def weights(z, sp, later, diagonal):
        w = jnp.exp2(z - sp + later)
        if diagonal:
            w = jnp.where(strictly_earlier, w, 0.0)
        return w.astype(BF16)

    def windows(blocks, nwins):
        qs, vws, zs = [], [], []
        for i, nwin in zip(blocks, nwins):
            qstart = i * CHUNK
            kstart = (i - nwin + 1) * CHUNK
            if not isinstance(i, int):
                qstart = pl.multiple_of(qstart, CHUNK)
                kstart = pl.multiple_of(kstart, CHUNK)
            q = q_ref[0, pl.ds(qstart, CHUNK), :]
            kw = k_ref[0, pl.ds(kstart, nwin * CHUNK), :]
            vws.append(v_ref[0, pl.ds(kstart, nwin * CHUNK), :])
            z = lax.dot_general(q, kw, (((1,), (1,)), ((), ())),
                                preferred_element_type=F32) * scale
            qs.append(q)
            zs.append([z[:, c * CHUNK:(c + 1) * CHUNK] for c in range(nwin)])
        stats = [[log_survival(zc, c == len(z) - 1) for c, zc in enumerate(z)] for z in zs]
        carries, ws = [], []
        for z, stat in zip(zs, stats):
            nwin = len(z)
            carry = None
            w = [None] * nwin
            for c in reversed(range(nwin)):
                sp, cs = stat[c]
                later = cs[:, :CHUNK] if carry is None else cs[:, :CHUNK] + carry
                w[c] = weights(z[c], sp, later, c == nwin - 1)
                carry = cs[:, CHUNK:] if carry is None else carry + cs[:, CHUNK:]
            carries.append(carry)
            ws.append(w[0] if nwin == 1 else jnp.concatenate(w, axis=1))
        accs = [jnp.dot(w, vw, preferred_element_type=F32) for w, vw in zip(ws, vws)]
        return list(zip(qs, carries, accs))

    def far_chunks(q, c0, carry, acc):
        def cond(s):
            c, carry, _ = s
            return jnp.logical_and(c >= 0, jnp.max(carry) > -EXP_ZERO * LOG2E)

        def body(s):
            c, carry, acc = s
            start = pl.multiple_of(c * CHUNK, CHUNK)
            z = lax.dot_general(q, k_ref[0, pl.ds(start, CHUNK), :], (((1,), (1,)), ((), ())),
                                preferred_element_type=F32) * scale
            sp, cs = log_survival(z, False)
            w = weights(z, sp, cs[:, :CHUNK] + carry, False)
            acc = acc + jnp.dot(w, v_ref[0, pl.ds(start, CHUNK), :], preferred_element_type=F32)
            return c - 1, carry + cs[:, CHUNK:], acc

        return lax.while_loop(cond, body, (c0, carry, acc))[2]

    def store(i, acc):
        start = i * CHUNK if isinstance(i, int) else pl.multiple_of(i * CHUNK, CHUNK)
        o_ref[0, pl.ds(start, CHUNK), :] = acc.astype(o_ref.dtype)

    first = list(range(WINDOW))
    for i, (_, _, acc) in zip(first, windows(first, [i + 1 for i in first])):
        store(i, acc)

    def group(g, _):
        blocks = [WINDOW + g * GROUP + b for b in range(GROUP)]
        done = windows(blocks, [WINDOW] * GROUP)
        worst = done[0][1]
        for _, carry, _ in done[1:]:
            worst = jnp.maximum(worst, carry)
        for i, (_, _, acc) in zip(blocks, done):
            store(i, acc)

        @pl.when(jnp.max(worst) > -EXP_ZERO * LOG2E)
        def _():
            for i, (q, carry, acc) in zip(blocks, done):
                store(i, far_chunks(q, i - WINDOW, carry, acc))

        return 0

    lax.fori_loop(0, (n_blocks - WINDOW) // GROUP, group, 0)


def _attention(qkv3, t_pad):
    b = qkv3.shape[0]
    n_blocks = t_pad // CHUNK
    assert n_blocks >= WINDOW and (n_blocks - WINDOW) % GROUP == 0
    blk = (1, t_pad, HEAD)
    return pl.pallas_call(
        functools.partial(_attn_kernel, n_blocks=n_blocks),
        grid=(b, N_HEADS),
        in_specs=[pl.BlockSpec(blk, lambda bi, h: (bi, 0, h)),
                  pl.BlockSpec(blk, lambda bi, h: (bi, 0, N_HEADS + h)),
                  pl.BlockSpec(blk, lambda bi, h: (bi, 0, 2 * N_HEADS + h))],
        out_specs=pl.BlockSpec(blk, lambda bi, h: (bi, 0, h)),
        out_shape=jax.ShapeDtypeStruct((b, t_pad, ATTN_WIDTH), F32),
        compiler_params=_cparams(("arbitrary", "arbitrary")),
        name="stickbreak_attn",
    )(qkv3, qkv3, qkv3)


def _lru_kernel(xr_ref, yg_ref, cw_ref, cb_ref, wa_ref, ba_ref, wi_ref, bi_ref,
                lam_ref, o_ref, x_ref, xc_ref, a_ref, u_ref, *, t_pad, groups):
    rows_p = 8 * groups
    x_ref[0:t_pad, :] = xr_ref[0]
    x_ref[t_pad:rows_p, :] = jnp.zeros((rows_p - t_pad, HEAD), F32)
    sub = lax.broadcasted_iota(jnp.int32, (8, HEAD), 0)
    load = lambda g: x_ref[pl.ds(g, 8, stride=groups), :]
    group = lambda g: slice(8 * g, 8 * g + 8)

    def chunk_down(v, fill):
        return jnp.where(sub >= 1, pltpu.roll(v, 1, 0), fill)

    w0, w1, w2, w3, bias = (jnp.broadcast_to(r, (8, HEAD)) for r in
                            (cw_ref[0:1, :], cw_ref[1:2, :], cw_ref[2:3, :], cw_ref[3:4, :],
                             cb_ref[...]))
    p3, p2, p1 = (chunk_down(load(groups - k), 0.0) for k in (3, 2, 1))
    for g in range(groups):
        cur = load(g)
        xc_ref[group(g), :] = cur * w3 + p1 * w2 + p2 * w1 + p3 * w0 + bias
        p3, p2, p1 = p2, p1, cur

    xc = xc_ref[...]
    xcb = xc.astype(BF16)
    r = jax.nn.sigmoid(jnp.dot(xcb, wa_ref[...].astype(BF16), preferred_element_type=F32)
                       + ba_ref[...])
    gi = jax.nn.sigmoid(jnp.dot(xcb, wi_ref[...].astype(BF16), preferred_element_type=F32)
                        + bi_ref[...])
    log_a = (-LRU_C) * r * _softplus(-lam_ref[...])
    a_ref[...] = jnp.exp(log_a)
    th = jnp.tanh(log_a)
    u_ref[...] = jnp.sqrt(-2.0 * th / (1.0 - th)) * (gi * xc)

    h = jnp.zeros((8, HEAD), F32)
    p = jnp.ones((8, HEAD), F32)
    for g in range(groups):
        a_g = a_ref[group(g), :]
        p = a_g * p
        h = a_g * h + u_ref[group(g), :]
        a_ref[group(g), :] = p
        u_ref[group(g), :] = h

    for d in (1, 2, 4):
        keep = sub >= d
        h = p * jnp.where(keep, pltpu.roll(h, d, 0), 0.0) + h
        p = p * jnp.where(keep, pltpu.roll(p, d, 0), 1.0)
    h_in = chunk_down(h, 0.0)
    for g in range(groups):
        x_ref[pl.ds(g, 8, stride=groups), :] = u_ref[group(g), :] + a_ref[group(g), :] * h_in

    o_ref[0] = x_ref[0:t_pad, :] * jax.nn.gelu(yg_ref[0])


def _lru_branch(lru3, cw, cb, wa, ba, wi, bi, lam, l, t_pad):
    b = lru3.shape[0]
    blk = (1, t_pad, HEAD)
    groups = t_pad // 8 + 1
    groups += 1 - groups % 2
    xr_off = 0
    yg_off = LRU_WIDTH // HEAD
    vspec = pl.BlockSpec((None, 1, HEAD), lambda bi_, h: (l, 0, h))
    mspec = pl.BlockSpec((None, None, HEAD, HEAD), lambda bi_, h: (l, h, 0, 0))
    return pl.pallas_call(
        functools.partial(_lru_kernel, t_pad=t_pad, groups=groups),
        grid=(b, N_HEADS),
        in_specs=[pl.BlockSpec(blk, lambda bi_, h: (bi_, 0, xr_off + h)),
                  pl.BlockSpec(blk, lambda bi_, h: (bi_, 0, yg_off + h)),
                  pl.BlockSpec((None, 4, HEAD), lambda bi_, h: (l, 0, h)),
                  vspec, mspec, vspec, mspec, vspec, vspec],
        out_specs=pl.BlockSpec(blk, lambda bi_, h: (bi_, 0, h)),
        out_shape=jax.ShapeDtypeStruct((b, t_pad, LRU_WIDTH), F32),
        scratch_shapes=[pltpu.VMEM((8 * groups, HEAD), F32)] * 4,
        compiler_params=_cparams(("arbitrary", "arbitrary")),
        name="rglru",
    )(lru3, lru3, cw, cb, wa, ba, wi, bi, lam)


def _rms_rows(x, g):
    ms = jnp.mean(x * x, axis=-1, keepdims=True)
    return x * lax.rsqrt(ms + RMS_EPS) * g


def _outproj_kernel(oa_ref, ol_ref, ga_ref, gl_ref, w_ref, h_ref, g_ref, b_ref, o_ref, ob_ref):
    na = _rms_rows(oa_ref[...], ga_ref[...]).astype(BF16)
    nl = _rms_rows(ol_ref[...], gl_ref[...]).astype(BF16)
    mix = (jnp.dot(na, w_ref[:ATTN_WIDTH, :], preferred_element_type=F32)
           + jnp.dot(nl, w_ref[ATTN_WIDTH:, :], preferred_element_type=F32))
    y = _layer_norm_rows(DEEPNORM_ALPHA * h_ref[...] + mix, g_ref[...], b_ref[...])
    o_ref[...] = y
    ob_ref[...] = y.astype(BF16)


def _outproj(oa, ol, ga, gl, w, l, h, g, b, bm):
    n, d = h.shape
    row = lambda wdt: pl.BlockSpec((bm, wdt), lambda i: (i, 0))
    vec = lambda wdt: pl.BlockSpec((None, 1, wdt), lambda i: (l, 0, 0))
    return pl.pallas_call(
        _outproj_kernel,
        grid=(n // bm,),
        in_specs=[row(ATTN_WIDTH), row(LRU_WIDTH), vec(ATTN_WIDTH), vec(LRU_WIDTH),
                  pl.BlockSpec((None, d, d), lambda i: (l, 0, 0)), row(d), vec(d), vec(d)],
        out_specs=(row(d), row(d)),
        out_shape=_stream_shapes(n, d),
        compiler_params=_cparams(("arbitrary",)),
        name="outproj_ln1",
    )(oa, ol, ga, gl, w, h, g, b)


def _ffn_up_kernel(x_ref, wg_ref, wv_ref, cwg_ref, cwv_ref, cbg_ref, cbv_ref, o_ref,
                   ug_ref, uv_ref, ug1_ref, uv1_ref, hg_ref, hv_ref, nat_ref, *,
                   nck, tiles_per_seq, pieces):
    s = pl.program_id(0)
    slabs, bm, lanes = ug_ref.shape
    groups = bm // 8

    @pl.when(s == 0)
    def _():
        ug_ref[...] = jnp.zeros_like(ug_ref)
        uv_ref[...] = jnp.zeros_like(uv_ref)
        hg_ref[...] = jnp.zeros_like(hg_ref)
        hv_ref[...] = jnp.zeros_like(hv_ref)

    prev = jnp.maximum(s - 1, 0)
    ip = prev // nck
    jp = prev % nck
    seq_start = (ip % tiles_per_seq) == 0
    sub = lax.broadcasted_iota(jnp.int32, (8, lanes), 0)

    def conv_taps(u_ref, tail_ref, w_ref, b_ref, j):
        cols = slice(j * lanes, (j + 1) * lanes)
        w0, w1, w2, bias = (jnp.broadcast_to(r, (8, lanes)) for r in
                            (w_ref[0:1, cols], w_ref[1:2, cols], w_ref[2:3, cols], b_ref[:, cols]))
        load = lambda g: u_ref[j, pl.ds(g, 8, stride=groups), :]
        last2, last1 = load(groups - 2), load(groups - 1)
        tail2 = jnp.where(seq_start, 0.0, tail_ref[jp, j, 0:8, :])
        tail1 = jnp.where(seq_start, 0.0, tail_ref[jp, j, 8:16, :])
        tail_ref[jp, j, 0:8, :] = last2
        tail_ref[jp, j, 8:16, :] = last1
        before2 = jnp.where(sub >= 1, pltpu.roll(last2, 1, 0), pltpu.roll(tail2, 1, 0))
        before1 = jnp.where(sub >= 1, pltpu.roll(last1, 1, 0), pltpu.roll(tail1, 1, 0))
        conv = lambda cur, b1, b2: cur * w2 + b1 * w1 + b2 * w0 + bias
        return load, conv, before1, before2

    def step(ugp_ref, uvp_ref, ugn_ref, uvn_ref):
        state = []
        for j in range(slabs):
            load_g, conv_g, g1, g2 = conv_taps(ugp_ref, hg_ref, cwg_ref, cbg_ref, j)
            load_v, conv_v, v1, v2 = conv_taps(uvp_ref, hv_ref, cwv_ref, cbv_ref, j)
            state.append([load_g, conv_g, g1, g2, load_v, conv_v, v1, v2])
        for p in range(pieces):
            for j in range(slabs):
                load_g, conv_g, g1, g2, load_v, conv_v, v1, v2 = state[j]
                for g in range(p * groups // pieces, (p + 1) * groups // pieces):
                    cur_g, cur_v = load_g(g), load_v(g)
                    act = jax.nn.silu(conv_g(cur_g, g1, g2)) * conv_v(cur_v, v1, v2)
                    nat_ref[j, pl.ds(g, 8, stride=groups), :] = act
                    g2, g1, v2, v1 = g1, cur_g, v1, cur_v
                state[j] = [load_g, conv_g, g1, g2, load_v, conv_v, v1, v2]
            rows = slice(p * bm // pieces, (p + 1) * bm // pieces)
            x = x_ref[rows, :]
            dg = jnp.dot(x, wg_ref[...], preferred_element_type=F32)
            dv = jnp.dot(x, wv_ref[...], preferred_element_type=F32)
            for j in range(slabs):
                ugn_ref[j, rows, :] = dg[:, j * lanes:(j + 1) * lanes]
                uvn_ref[j, rows, :] = dv[:, j * lanes:(j + 1) * lanes]
        for j in range(slabs):
            o_ref[:, j * lanes:(j + 1) * lanes] = nat_ref[j].astype(o_ref.dtype)

    @pl.when(s % 2 == 0)
    def _():
        step(ug_ref, uv_ref, ug1_ref, uv1_ref)

    @pl.when(s % 2 == 1)
    def _():
        step(ug1_ref, uv1_ref, ug_ref, uv_ref)


def _ffn_up(hb, w_up, conv_w, conv_b, l, bm, ck, tiles_per_seq):
    n, d = hb.shape
    nck = FFN_HIDDEN // ck
    slabs = ck // 128
    pieces = max(p for p in range(1, 4) if bm % (16 * p) == 0 and (bm // p >= 128 or p == 1))
    steps = (n // bm) * nck
    cur = lambda s: jnp.minimum(s, steps - 1)
    prev = lambda s: jnp.maximum(s - 1, 0)
    return pl.pallas_call(
        functools.partial(_ffn_up_kernel, nck=nck, tiles_per_seq=tiles_per_seq, pieces=pieces),
        grid=(steps + 1,),
        in_specs=[pl.BlockSpec((bm, d), lambda s: (cur(s) // nck, 0)),
                  pl.BlockSpec((None, d, ck), lambda s: (l, 0, cur(s) % nck)),
                  pl.BlockSpec((None, d, ck), lambda s: (l, 0, nck + cur(s) % nck)),
                  pl.BlockSpec((None, 3, ck), lambda s: (l, 0, prev(s) % nck)),
                  pl.BlockSpec((None, 3, ck), lambda s: (l, 0, nck + prev(s) % nck)),
                  pl.BlockSpec((None, 1, ck), lambda s: (l, 0, prev(s) % nck)),
                  pl.BlockSpec((None, 1, ck), lambda s: (l, 0, nck + prev(s) % nck))],
        out_specs=pl.BlockSpec((bm, ck), lambda s: (prev(s) // nck, prev(s) % nck)),
        out_shape=jax.ShapeDtypeStruct((n, FFN_HIDDEN), BF16),
        scratch_shapes=[pltpu.VMEM((slabs, bm, 128), F32),
                        pltpu.VMEM((slabs, bm, 128), F32),
                        pltpu.VMEM((slabs, bm, 128), F32),
                        pltpu.VMEM((slabs, bm, 128), F32),
                        pltpu.VMEM((nck, slabs, 16, 128), F32),
                        pltpu.VMEM((nck, slabs, 16, 128), F32),
                        pltpu.VMEM((slabs, bm, 128), F32)],
        compiler_params=_cparams(("arbitrary",)),
        name="ffn_up_conv_glu",
    )(hb, w_up, w_up, conv_w, conv_w, conv_b, conv_b)


def _ffn_down_kernel(a_ref, w_ref, h_ref, g_ref, b_ref, o_ref, ob_ref):
    o_ref[...] = jnp.dot(a_ref[...], w_ref[...], preferred_element_type=F32)
    for r in range(0, o_ref.shape[0], LN_SLAB):
        rows = pl.ds(r, LN_SLAB)
        y = _layer_norm_rows(DEEPNORM_ALPHA * h_ref[rows, :] + o_ref[rows, :],
                             g_ref[...], b_ref[...])
        o_ref[rows, :] = y
        ob_ref[rows, :] = y.astype(BF16)


def _ffn_down(act, w, l, h, g, b, bm):
    n, d = h.shape
    assert bm % LN_SLAB == 0 and n % bm == 0
    row = pl.BlockSpec((bm, d), lambda i: (i, 0))
    vec = pl.BlockSpec((None, 1, d), lambda i: (l, 0, 0))
    return pl.pallas_call(
        _ffn_down_kernel,
        grid=(n // bm,),
        in_specs=[pl.BlockSpec((bm, FFN_HIDDEN), lambda i: (i, 0)),
                  pl.BlockSpec((None, FFN_HIDDEN, d), lambda i: (l, 0, 0),
                               pipeline_mode=pl.Buffered(1)),
                  row, vec, vec],
        out_specs=(row, row),
        out_shape=_stream_shapes(n, d),
        compiler_params=_cparams(("arbitrary",)),
        name="ffn_down_ln2",
    )(act, w, h, g, b)


def kernel(x, meta_tokens, ln0_g, ln0_b, w_in, lru_conv_w, lru_conv_b, lru_w_a, lru_b_a, lru_w_i, lru_b_i, lru_lambda, g_attn, g_lru, w_out, ln1_g, ln1_b, ffn_w_up, ffn_conv_w, ffn_conv_b, ffn_w_down, ln2_g, ln2_b):
    b, seq, d = x.shape
    t = N_META + seq
    t_pad = -(-t // CHUNK) * CHUNK
    tiles_per_seq = 4
    bm = t_pad // tiles_per_seq
    n = b * t_pad

    meta = jnp.broadcast_to(meta_tokens[None].astype(x.dtype), (b, N_META, d))
    tail = jnp.zeros((b, t_pad - t, d), x.dtype)
    h = jnp.concatenate([meta, x, tail], axis=1).reshape(n, d)
    h, hb = _ln_rows(h, ln0_g, ln0_b, bm // 2)

    vec = lambda a: a.reshape(DEPTH, 1, -1)
    w_in_b, w_out_b = w_in.astype(BF16), w_out.astype(BF16)
    w_up_b, w_down_b = ffn_w_up.astype(BF16), ffn_w_down.astype(BF16)
    lru_vecs = [vec(a) for a in (lru_conv_b, lru_b_a, lru_b_i, lru_lambda)]
    g_attn, g_lru, ln1_g, ln1_b, ln2_g, ln2_b, ffn_conv_b = (
        vec(a) for a in (g_attn, g_lru, ln1_g, ln1_b, ln2_g, ln2_b, ffn_conv_b))

    for l in range(DEPTH):
        qkv, lru_in = _inproj(hb, w_in_b, l, bm, 1024)
        o_attn = _attention(qkv.reshape(b, t_pad, 3 * ATTN_WIDTH), t_pad)
        o_lru = _lru_branch(lru_in.reshape(b, t_pad, 2 * LRU_WIDTH), lru_conv_w, lru_vecs[0],
                            lru_w_a, lru_vecs[1], lru_w_i, lru_vecs[2], lru_vecs[3], l, t_pad)
        h, hb = _outproj(o_attn.reshape(n, ATTN_WIDTH), o_lru.reshape(n, LRU_WIDTH),
                         g_attn, g_lru, w_out_b, l, h, ln1_g, ln1_b, bm // 2)
        act = _ffn_up(hb, w_up_b, ffn_conv_w, ffn_conv_b, l, bm, 512, tiles_per_seq)
        h, hb = _ffn_down(act, w_down_b, l, h, ln2_g, ln2_b, bm // 3)

    return h.reshape(b, t_pad, d)[:, N_META:t]
```
